```python
import math
import jax, jax.numpy as jnp
from jax import lax
import numpy as np

D_MODEL = 1024
BATCH = 4
SEQ = 8192
DEPTH = 2

HEAD_DIM = 64
A_GROUPS = 4
A_WIDTH = A_GROUPS * HEAD_DIM
A_CHUNK = 128
B_Q_HEADS = 8
B_KV_HEADS = 2
B_GROUP = B_Q_HEADS // B_KV_HEADS
B_WIDTH = B_Q_HEADS * HEAD_DIM
B_KV_WIDTH = B_KV_HEADS * HEAD_DIM
WINDOW = 128
REL_BUCKETS = 32
REL_MAX_DIST = 128
C_HEADS = 4
C_KEY_DIM = 64
C_VAL_DIM = 64
C_KEY_WIDTH = C_HEADS * C_KEY_DIM
C_WIDTH = C_HEADS * C_VAL_DIM
C_CHUNK = 16
MIX_WIDTH = A_WIDTH + B_WIDTH + C_WIDTH
IN_WIDTH = 2 * A_WIDTH + B_WIDTH + 2 * B_KV_WIDTH + 2 * C_KEY_WIDTH + 2 * C_WIDTH
D_FF = 2816
CONV_WIDTH = 3
EPS = 1e-6
MASK_VALUE = -1e30

kernel_name = "hybrid_gmlp_swa_hgrn2_block"


def rms_norm(x, g):
    xf = x.astype(jnp.float32)
    y = xf * lax.rsqrt(jnp.mean(xf * xf, axis=-1, keepdims=True) + EPS)
    return (y * g.astype(jnp.float32)).astype(x.dtype)


def spatial_gating_mixer(u, v, vnorm_g, w_s, b_s):
    B, S = u.shape[:2]
    n_chunks = S // A_CHUNK
    v = rms_norm(v, vnorm_g)
    causal = jnp.tril(jnp.ones((A_CHUNK, A_CHUNK), dtype=bool))
    w = jnp.where(causal, w_s, jnp.zeros_like(w_s))
    vc = v.reshape(B, n_chunks, A_CHUNK, A_GROUPS, HEAD_DIM)
    sv = jnp.einsum('gts,bcsgd->bctgd', w, vc) + b_s.T[:, :, None]
    return (u.reshape(vc.shape) * sv).reshape(B, S, A_WIDTH)


def t5_causal_bucket(dist):
    max_exact = REL_BUCKETS // 2
    n = jnp.maximum(dist, 0)
    is_small = n < max_exact
    nf = jnp.maximum(n, 1).astype(jnp.float32)
    large = max_exact + (jnp.log(nf / max_exact) / math.log(REL_MAX_DIST / max_exact)
                         * (REL_BUCKETS - max_exact)).astype(jnp.int32)
    large = jnp.minimum(large, REL_BUCKETS - 1)
    return jnp.where(is_small, n, large)


def sliding_window_attention(q, k, v, qn_g, kn_g, sinks, rel_bias):
    f32 = jnp.float32
    B, S = q.shape[:2]
    nb = S // WINDOW
    q = rms_norm(q, qn_g).astype(f32)
    k = rms_norm(k, kn_g).astype(f32)
    v = v.astype(f32)
    qb = q.reshape(B, nb, WINDOW, B_KV_HEADS, B_GROUP, HEAD_DIM)

    def banded(t):
        tb = t.reshape(B, nb, WINDOW, B_KV_HEADS, HEAD_DIM)
        prev = jnp.concatenate([jnp.zeros_like(tb[:, :1]), tb[:, :-1]], axis=1)
        return jnp.concatenate([prev, tb], axis=2)

    kw, vw = banded(k), banded(v)
    scores = jnp.einsum('bnihgd,bnjhd->bhgnij', qb, kw) * (HEAD_DIM ** -0.5)
    qi = jnp.arange(WINDOW)[:, None]
    kj = jnp.arange(2 * WINDOW)[None, :]
    dist = qi + WINDOW - kj
    bias = rel_bias.astype(f32)[t5_causal_bucket(dist)]
    bias = bias.reshape(WINDOW, 2 * WINDOW, B_KV_HEADS, B_GROUP).transpose(2, 3, 0, 1)[:, :, None]
    blk = jnp.arange(nb)[:, None, None]
    valid = (dist >= 0) & (dist < WINDOW) & (blk * WINDOW - WINDOW + kj >= 0)
    scores = jnp.where(valid, scores + bias, MASK_VALUE)
    sink = sinks.astype(f32).reshape(B_KV_HEADS, B_GROUP, 1, 1, 1)
    m = jnp.maximum(scores.max(axis=-1, keepdims=True), sink)
    p = jnp.exp(scores - m)
    denom = p.sum(axis=-1, keepdims=True) + jnp.exp(sink - m)
    o = jnp.einsum('bhgnij,bnjhd->bnihgd', p / denom, vw)
    return o.reshape(B, S, B_WIDTH)


def hgrn2_mixer(q, fz, i, g, lb, onorm_g):
    f32 = jnp.float32
    B, S = q.shape[:2]
    N = S // C_CHUNK
    lb = lb.astype(f32)
    z = fz.astype(f32)
    q = jax.nn.silu(q.astype(f32))
    log_f = jnp.logaddexp(jnp.log(lb), jnp.log1p(-lb) + jax.nn.log_sigmoid(z))
    k = (1.0 - lb) * jax.nn.sigmoid(-z)
    shp = (B, N, C_CHUNK, C_HEADS, C_KEY_DIM)
    q, k, log_f = q.reshape(shp), k.reshape(shp), log_f.reshape(shp)
    v = i.astype(f32).reshape(B, N, C_CHUNK, C_HEADS, C_VAL_DIM)
    cum = jnp.cumsum(log_f, axis=2)
    causal = jnp.tril(jnp.ones((C_CHUNK, C_CHUNK), dtype=bool))[:, :, None, None]
    rel = jnp.exp(jnp.where(causal, cum[:, :, :, None] - cum[:, :, None, :], -jnp.inf))
    attn = jnp.einsum('bnthk,bntshk,bnshk->bnhts', q, rel, k)
    o = jnp.einsum('bnhts,bnshv->bnthv', attn, v)
    last = cum[:, :, -1]
    w_k = k * jnp.exp(last[:, :, None] - cum)
    dstate = jnp.einsum('bnshk,bnshv->nbhkv', w_k, v)
    decay = jnp.exp(last).transpose(1, 0, 2, 3)

    def step(state, inp):
        dec, ds = inp
        return dec[..., None] * state + ds, state

    init = jnp.zeros((B, C_HEADS, C_KEY_DIM, C_VAL_DIM), f32)
    _, s_in = lax.scan(step, init, (decay, dstate))
    o = o + jnp.einsum('bnthk,nbhkv->bnthv', q * jnp.exp(cum), s_in)
    o = o.reshape(B, S, C_HEADS, C_VAL_DIM)
    o = rms_norm(o, onorm_g) * jax.nn.silu(g.astype(f32))
    return o.reshape(B, S, C_WIDTH)


def conv_ffn(h, w_gate, w_up, conv_w, conv_b, w_down):
    S = h.shape[1]
    gate = h @ w_gate
    gp = jnp.pad(gate, ((0, 0), (CONV_WIDTH - 1, 0), (0, 0)))
    conv = conv_b
    for tap in range(CONV_WIDTH):
        conv = conv + conv_w[tap] * gp[:, tap:tap + S]
    return (jax.nn.silu(conv) * (h @ w_up)) @ w_down


def setup_inputs(seed: int = 0) -> dict:
    key = jax.random.key(seed)
    ks = jax.random.split(key, 20)
    nrm = jax.random.normal
    f32 = jnp.float32
    return {
        "x": nrm(ks[0], (BATCH, SEQ, D_MODEL), f32),
        "norm1_g": 1.0 + 0.02 * nrm(ks[1], (DEPTH, D_MODEL), f32),
        "w_in": nrm(ks[2], (DEPTH, D_MODEL, IN_WIDTH), f32) * D_MODEL ** -0.5,
        "gmlp_vnorm_g": 1.0 + 0.02 * nrm(ks[3], (DEPTH, A_GROUPS, HEAD_DIM), f32),
        "gmlp_w_s": nrm(ks[4], (DEPTH, A_GROUPS, A_CHUNK, A_CHUNK), f32) * A_CHUNK ** -0.5,
        "gmlp_b_s": 1.0 + 0.02 * nrm(ks[5], (DEPTH, A_GROUPS, A_CHUNK), f32),
        "q_norm_g": 1.0 + 0.02 * nrm(ks[6], (DEPTH, HEAD_DIM), f32),
        "k_norm_g": 1.0 + 0.02 * nrm(ks[7], (DEPTH, HEAD_DIM), f32),
        "attn_sinks": 0.5 * nrm(ks[8], (DEPTH, B_Q_HEADS), f32),
        "rel_bias": 0.5 * nrm(ks[9], (REL_BUCKETS, B_Q_HEADS), f32),
        "hgrn_lb_logits": nrm(ks[10], (DEPTH, C_KEY_WIDTH), f32),
        "hgrn_onorm_g": 1.0 + 0.02 * nrm(ks[11], (DEPTH, C_VAL_DIM), f32),
        "w_out": nrm(ks[12], (DEPTH, MIX_WIDTH, D_MODEL), f32) * MIX_WIDTH ** -0.5,
        "norm2_g": 1.0 + 0.02 * nrm(ks[13], (DEPTH, D_MODEL), f32),
        "w_gate": nrm(ks[14], (DEPTH, D_MODEL, D_FF), f32) * D_MODEL ** -0.5,
        "w_up": nrm(ks[15], (DEPTH, D_MODEL, D_FF), f32) * D_MODEL ** -0.5,
        "conv_w": nrm(ks[16], (DEPTH, CONV_WIDTH, D_FF), f32) * CONV_WIDTH ** -0.5,
        "conv_b": 0.02 * nrm(ks[17], (DEPTH, D_FF), f32),
        "w_down": nrm(ks[18], (DEPTH, D_FF, D_MODEL), f32) * D_FF ** -0.5,
    }


def reference(x, norm1_g, w_in, gmlp_vnorm_g, gmlp_w_s, gmlp_b_s, q_norm_g, k_norm_g,
              attn_sinks, rel_bias, hgrn_lb_logits, hgrn_onorm_g, w_out, norm2_g,
              w_gate, w_up, conv_w, conv_b, w_down):
    B, S, _ = x.shape
    widths = [A_WIDTH, A_WIDTH, B_WIDTH, B_KV_WIDTH, B_KV_WIDTH,
              C_KEY_WIDTH, C_KEY_WIDTH, C_WIDTH, C_WIDTH]
    split_idx = np.cumsum(widths)[:-1].tolist()
    lb_cum = jnp.cumsum(jax.nn.softmax(hgrn_lb_logits.astype(jnp.float32), axis=0), axis=0)
    lower_bounds = lb_cum - lb_cum[0]
    for l in range(DEPTH):
        h = rms_norm(x, norm1_g[l])
        proj = h @ w_in[l]
        a_u, a_v, b_q, b_k, b_v, c_q, c_f, c_i, c_g = jnp.split(proj, split_idx, axis=-1)
        u = jax.nn.gelu(a_u, approximate=False).reshape(B, S, A_GROUPS, HEAD_DIM)
        v = jax.nn.gelu(a_v, approximate=False).reshape(B, S, A_GROUPS, HEAD_DIM)
        y_a = spatial_gating_mixer(u, v, gmlp_vnorm_g[l], gmlp_w_s[l], gmlp_b_s[l])
        y_b = sliding_window_attention(
            b_q.reshape(B, S, B_Q_HEADS, HEAD_DIM),
            b_k.reshape(B, S, B_KV_HEADS, HEAD_DIM),
            b_v.reshape(B, S, B_KV_HEADS, HEAD_DIM),
            q_norm_g[l], k_norm_g[l], attn_sinks[l], rel_bias)
        y_c = hgrn2_mixer(
            c_q.reshape(B, S, C_HEADS, C_KEY_DIM),
            c_f.reshape(B, S, C_HEADS, C_KEY_DIM),
            c_i.reshape(B, S, C_HEADS, C_VAL_DIM),
            c_g.reshape(B, S, C_HEADS, C_VAL_DIM),
            lower_bounds[l].reshape(C_HEADS, C_KEY_DIM), hgrn_onorm_g[l])
        mixed = jnp.concatenate([y_a.astype(x.dtype), y_b.astype(x.dtype), y_c.astype(x.dtype)], axis=-1)
        x = x + mixed @ w_out[l]
        h = rms_norm(x, norm2_g[l])
        x = x + conv_ffn(h, w_gate[l], w_up[l], conv_w[l], conv_b[l], w_down[l])
    return x
```

```python
import functools
import math

import numpy as np
import jax
import jax.numpy as jnp
from jax import lax
from jax.experimental import pallas as pl
from jax.experimental.pallas import tpu as pltpu

F32 = jnp.float32
BF16 = jnp.bfloat16

D_MODEL = 1024
HEAD_DIM = 64
CHUNK = 128
A_GROUPS = 4
A_WIDTH = 256
B_Q_HEADS = 8
B_KV_HEADS = 2
B_GROUP = 4
B_WIDTH = 512
B_KV_WIDTH = 128
REL_BUCKETS = 32
REL_MAX_DIST = 128
C_HEADS = 4
C_WIDTH = 256
IN_WIDTH = 2304
D_FF = 2816
CONV_WIDTH = 3
EPS = 1e-6
MASK_VALUE = -1e30

OFF_AU, OFF_AV, OFF_BQ, OFF_BK, OFF_BV = 0, 256, 512, 1024, 1152
OFF_CQ, OFF_CF, OFF_CI, OFF_CG = 1280, 1536, 1792, 2048

HGRN_LEVELS = (64, 32, 16, 8, 4, 2, 1)

MIXER_ROWS = 256
FFN_ROWS = 256
VMEM_LIMIT_BYTES = 56 * 1024 * 1024


def _dot(a, b):
    return jnp.dot(a, b, preferred_element_type=F32)


def _dot_nt(a, b):
    return lax.dot_general(a, b, (((1,), (1,)), ((), ())), preferred_element_type=F32)


def _split3(x):
    hi = x.astype(BF16)
    r = x - hi.astype(F32)
    mid = r.astype(BF16)
    lo = (r - mid.astype(F32)).astype(BF16)
    return hi, mid, lo


def _seg_mean_sq(x, bd):
    sq = x * x
    hi = sq.astype(BF16)
    lo = (sq - hi.astype(F32)).astype(BF16)
    w = x.shape[-1]
    outs = []
    for c0 in range(0, w, 256):
        cw = min(256, w - c0)
        m = bd[0:cw, 0:cw]
        outs.append(_dot(hi[:, c0:c0 + cw], m) + _dot(lo[:, c0:c0 + cw], m))
    out = outs[0] if len(outs) == 1 else jnp.concatenate(outs, axis=-1)
    return out * (1.0 / HEAD_DIM)


def _sigmoid(x):
    return 1.0 / (1.0 + jnp.exp(-x))


def _gelu(x):
    return 0.5 * x * (1.0 + lax.erf(x * (1.0 / math.sqrt(2.0))))


def _rms_rows(x, g):
    ms = jnp.mean(x * x, axis=-1, keepdims=True)
    return x * lax.rsqrt(ms + EPS) * g


def _gmlp_chunk(u, vb, aw_ref, ab, lane_group):
    sv = jnp.zeros((CHUNK, A_WIDTH), F32)
    for g in range(A_GROUPS):
        r = _dot(aw_ref[g], vb)
        sv = jnp.where(lane_group == g, r, sv)
    return u * (sv + ab)


def _attn_chunk(qb, kk, vk, bias_ref, bias_idx, sink_ref):
    outs = []
    for h in range(B_Q_HEADS):
        g = h // B_GROUP
        qh = qb[:, h * HEAD_DIM:(h + 1) * HEAD_DIM]
        kh = kk[:, g * HEAD_DIM:(g + 1) * HEAD_DIM]
        vh = vk[:, g * HEAD_DIM:(g + 1) * HEAD_DIM]
        s = _dot_nt(qh, kh) + bias_ref[bias_idx, h]
        sink = sink_ref[h]
        m = jnp.maximum(jnp.max(s, axis=-1, keepdims=True), sink)
        p = jnp.exp(s - m)
        den = jnp.sum(p, axis=-1, keepdims=True) + jnp.exp(sink - m)
        o = _dot(p.astype(BF16), vh)
        outs.append(o * (1.0 / den))
    return jnp.concatenate(outs, axis=-1)


def _hgrn_chunk(cq, cf, ci, cg, st, prm, tri, lvl_ref, lane_head, row_id, bd):
    log_lb, log_1m_lb, one_m_lb, og = prm
    qs = cq * _sigmoid(cq)
    ls = jnp.minimum(cf, 0.0) - jnp.log1p(jnp.exp(-jnp.abs(cf)))
    bq = log_1m_lb + ls
    mx = jnp.maximum(log_lb, bq)
    mn = jnp.minimum(log_lb, bq)
    lf = mx + jnp.log1p(jnp.exp(mn - mx))
    kin = one_m_lb * _sigmoid(-cf)
    hi, mid, lo = _split3(lf)
    b = _dot(tri, hi) + _dot(tri, mid) + _dot(tri, lo)
    total = b[CHUNK - 1:CHUNK, :]

    vb = ci.astype(BF16)
    zero_b = jnp.zeros_like(vb)
    heads = range(C_HEADS)

    def head_stack(xb):
        return jnp.concatenate([jnp.where(lane_head == h, xb, zero_b) for h in heads], axis=0)

    lvl = lvl_ref[...]
    a = jnp.where(lvl == 0, _dot_nt(qs.astype(BF16), head_stack(kin.astype(BF16))), 0.0)
    for li, half in enumerate(HGRN_LEVELS):
        blk = 2 * half
        if blk >= 8:
            pieces = []
            for j in range(CHUNK // blk):
                r = j * blk + half - 1
                pieces.append(jnp.broadcast_to(b[r:r + 1, :], (blk, C_WIDTH)))
            bref = pieces[0] if len(pieces) == 1 else jnp.concatenate(pieces, axis=0)
        else:
            pos = row_id % blk
            bref = b
            for p in range(blk):
                shift = p - (half - 1)
                if shift == 0:
                    continue
                rolled = pltpu.roll(b, shift % CHUNK, axis=0)
                bref = jnp.where(pos == p, rolled, bref)
        e = jnp.exp(-jnp.abs(b - bref))
        s = _dot_nt((qs * e).astype(BF16), head_stack((kin * e).astype(BF16)))
        a = jnp.where(lvl == li + 1, s, a)
    o = _dot(a.astype(BF16), head_stack(vb))
    o = o + _dot_nt((qs * jnp.exp(b)).astype(BF16), st.astype(BF16))
    kdec = (kin * jnp.exp(total - b)).astype(BF16)
    upd = _dot(ci.T.astype(BF16), kdec)
    vrow_head = lax.broadcasted_iota(jnp.int32, (C_WIDTH, C_WIDTH), 0) // HEAD_DIM
    kcol_head = lax.broadcasted_iota(jnp.int32, (C_WIDTH, C_WIDTH), 1) // HEAD_DIM
    st = st * jnp.exp(total) + jnp.where(vrow_head == kcol_head, upd, 0.0)
    o = o * lax.rsqrt(_seg_mean_sq(o, bd) + EPS) * og
    return o * (cg * _sigmoid(cg)), st


def _mixer_kernel(x_ref, g1_ref, win_ref, wout_ref, bd_ref, aw_ref, ab_ref, ag_ref, qg_ref, kg_ref,
                  bias_ref, sink_ref, lb_ref, og_ref, lvl_ref, o_ref, kv_ref, st_ref, *, rows):
    j = pl.program_id(1)

    @pl.when(j == 0)
    def _():
        kv_ref[...] = jnp.zeros_like(kv_ref)
        st_ref[...] = jnp.zeros_like(st_ref)

    x = x_ref[0]
    h = _rms_rows(x, g1_ref[...]).astype(BF16)
    proj = _dot(h, win_ref[...])
    bd = bd_ref[...]

    u = _gelu(proj[:, OFF_AU:OFF_AU + A_WIDTH])
    v = _gelu(proj[:, OFF_AV:OFF_AV + A_WIDTH])
    v = (v * lax.rsqrt(_seg_mean_sq(v, bd) + EPS) * ag_ref[...]).astype(BF16)
    ab = ab_ref[...]
    lane_group = lax.broadcasted_iota(jnp.int32, (CHUNK, A_WIDTH), 1) // HEAD_DIM

    q = proj[:, OFF_BQ:OFF_BQ + B_WIDTH]
    q = (q * lax.rsqrt(_seg_mean_sq(q, bd) + EPS) * qg_ref[...]).astype(BF16)
    k = proj[:, OFF_BK:OFF_BK + B_KV_WIDTH]
    k = (k * lax.rsqrt(_seg_mean_sq(k, bd) + EPS) * kg_ref[...]).astype(BF16)
    vv = proj[:, OFF_BV:OFF_BV + B_KV_WIDTH].astype(BF16)
    k_prev = kv_ref[:, 0:B_KV_WIDTH]
    v_prev = kv_ref[:, B_KV_WIDTH:2 * B_KV_WIDTH]

    lb = lb_ref[...]
    prm = (jnp.log(lb), jnp.log1p(-lb), 1.0 - lb, og_ref[...])
    tri = (lax.broadcasted_iota(jnp.int32, (CHUNK, CHUNK), 0)
           >= lax.broadcasted_iota(jnp.int32, (CHUNK, CHUNK), 1)).astype(BF16)
    lane_head = lax.broadcasted_iota(jnp.int32, (CHUNK, C_WIDTH), 1) // HEAD_DIM
    row_id = lax.broadcasted_iota(jnp.int32, (CHUNK, C_WIDTH), 0)
    st = st_ref[...]

    mixed = []
    for c in range(rows // CHUNK):
        r0 = c * CHUNK
        sl = slice(r0, r0 + CHUNK)
        ya = _gmlp_chunk(u[sl], v[sl], aw_ref, ab, lane_group)
        kc, vc = k[sl], vv[sl]
        first = jnp.logical_and(j == 0, c == 0)
        bias_idx = jnp.where(first, 0, 1)
        yb = _attn_chunk(q[sl], jnp.concatenate([k_prev, kc], axis=0),
                         jnp.concatenate([v_prev, vc], axis=0), bias_ref, bias_idx, sink_ref)
        k_prev, v_prev = kc, vc
        yc, st = _hgrn_chunk(proj[sl, OFF_CQ:OFF_CQ + C_WIDTH], proj[sl, OFF_CF:OFF_CF + C_WIDTH],
                             proj[sl, OFF_CI:OFF_CI + C_WIDTH], proj[sl, OFF_CG:OFF_CG + C_WIDTH],
                             st, prm, tri, lvl_ref, lane_head, row_id, bd)
        mixed.append(jnp.concatenate([ya, yb, yc], axis=-1).astype(BF16))
    mixed = mixed[0] if len(mixed) == 1 else jnp.concatenate(mixed, axis=0)
    o_ref[0] = x + _dot(mixed, wout_ref[...])
    kv_ref[:, 0:B_KV_WIDTH] = k_prev
    kv_ref[:, B_KV_WIDTH:2 * B_KV_WIDTH] = v_prev
    st_ref[...] = st


def _const_spec(shape):
    nd = len(shape)
    return pl.BlockSpec(shape, lambda b, j, _nd=nd: (0,) * _nd, pipeline_mode=pl.Buffered(1))


def _mixer_layer(x, p, rows):
    bsz, seq, _ = x.shape
    grid = (bsz, seq // rows)
    x_spec = pl.BlockSpec((1, rows, D_MODEL), lambda b, j: (b, j, 0))
    consts = [p["g1"], p["w_in"], p["w_out"], p["bd"], p["a_w"], p["a_b"], p["a_g"], p["q_g"], p["k_g"],
              p["bias"]]
    tail = [p["lb"], p["o_g"], p["lvl"]]
    in_specs = ([x_spec] + [_const_spec(c.shape) for c in consts]
                + [pl.BlockSpec(memory_space=pltpu.SMEM)] + [_const_spec(c.shape) for c in tail])
    return pl.pallas_call(
        functools.partial(_mixer_kernel, rows=rows),
        grid=grid,
        in_specs=in_specs,
        out_specs=x_spec,
        out_shape=jax.ShapeDtypeStruct(x.shape, x.dtype),
        scratch_shapes=[pltpu.VMEM((CHUNK, 2 * B_KV_WIDTH), BF16),
                        pltpu.VMEM((C_WIDTH, C_WIDTH), F32)],
        compiler_params=pltpu.CompilerParams(
            dimension_semantics=("arbitrary", "arbitrary"), vmem_limit_bytes=VMEM_LIMIT_BYTES),
        name="mixer_layer",
    )(x, *consts, p["sinks"], *tail)


def _ffn_kernel(x_ref, g2_ref, wg_ref, wu_ref, cw_ref, cb_ref, wd_ref, o_ref, gate_ref, *, rows):
    j = pl.program_id(1)

    @pl.when(j == 0)
    def _():
        gate_ref[0:8, :] = jnp.zeros((8, D_FF), F32)

    x = x_ref[0]
    h = _rms_rows(x, g2_ref[...]).astype(BF16)
    gate = _dot(h, wg_ref[...])
    gate_ref[8:8 + rows, :] = gate
    up = _dot(h, wu_ref[...])
    cw = cw_ref[...]
    conv = (cb_ref[...] + cw[2:3, :] * gate + cw[1:2, :] * gate_ref[7:7 + rows, :]
            + cw[0:1, :] * gate_ref[6:6 + rows, :])
    act = (conv * _sigmoid(conv) * up).astype(BF16)
    o_ref[0] = x + _dot(act, wd_ref[...])
    gate_ref[0:8, :] = gate_ref[rows:rows + 8, :]


def _ffn_layer(x, p, rows):
    bsz, seq, _ = x.shape
    grid = (bsz, seq // rows)
    x_spec = pl.BlockSpec((1, rows, D_MODEL), lambda b, j: (b, j, 0))
    consts = [p["g2"], p["w_gate"], p["w_up"], p["conv_w"], p["conv_b"], p["w_down"]]
    return pl.pallas_call(
        functools.partial(_ffn_kernel, rows=rows),
        grid=grid,
        in_specs=[x_spec] + [_const_spec(c.shape) for c in consts],
        out_specs=x_spec,
        out_shape=jax.ShapeDtypeStruct(x.shape, x.dtype),
        scratch_shapes=[pltpu.VMEM((rows + 8, D_FF), F32)],
        compiler_params=pltpu.CompilerParams(
            dimension_semantics=("arbitrary", "arbitrary"), vmem_limit_bytes=VMEM_LIMIT_BYTES),
        name="ffn_layer",
    )(x, *consts)


def _t5_causal_bucket(dist):
    max_exact = REL_BUCKETS // 2
    n = jnp.maximum(dist, 0)
    is_small = n < max_exact
    nf = jnp.maximum(n, 1).astype(F32)
    large = max_exact + (jnp.log(nf / max_exact) / math.log(REL_MAX_DIST / max_exact)
                         * (REL_BUCKETS - max_exact)).astype(jnp.int32)
    large = jnp.minimum(large, REL_BUCKETS - 1)
    return jnp.where(is_small, n, large)


def _attention_bias_tables(rel_bias):
    qi = jnp.arange(CHUNK)[:, None]
    kj = jnp.arange(2 * CHUNK)[None, :]
    dist = qi + CHUNK - kj
    bias = rel_bias.astype(F32)[_t5_causal_bucket(dist)]
    bias = jnp.transpose(bias, (2, 0, 1))
    valid = (dist >= 0) & (dist < CHUNK)
    rest = jnp.where(valid[None], bias, MASK_VALUE)
    first = jnp.where((valid & (kj >= CHUNK))[None], bias, MASK_VALUE)
    return jnp.stack([first, rest])


def _level_table():
    t = np.arange(CHUNK)[:, None]
    s = np.arange(CHUNK)[None, :]
    x = np.bitwise_xor(t, s)
    msb = np.floor(np.log2(np.maximum(x, 1))).astype(np.int32)
    lvl = np.where(s == t, 0, np.where(s < t, len(HGRN_LEVELS) - msb, -1)).astype(np.int32)
    return np.tile(lvl, (1, C_HEADS))


def _block_diag_ones():
    i = np.arange(256)
    return (i[:, None] // HEAD_DIM == i[None, :] // HEAD_DIM).astype(np.float32)


def kernel(x, norm1_g, w_in, gmlp_vnorm_g, gmlp_w_s, gmlp_b_s, q_norm_g, k_norm_g, attn_sinks, rel_bias,
           hgrn_lb_logits, hgrn_onorm_g, w_out, norm2_g, w_gate, w_up, conv_w, conv_b, w_down):
    depth = w_in.shape[0]
    seq = x.shape[1]
    mixer_rows = min(MIXER_ROWS, seq)
    ffn_rows = min(FFN_ROWS, seq)
    lb_cum = jnp.cumsum(jax.nn.softmax(hgrn_lb_logits.astype(F32), axis=0), axis=0)
    lower_bounds = lb_cum - lb_cum[0]
    bias_tables = _attention_bias_tables(rel_bias)
    causal = jnp.tril(jnp.ones((CHUNK, CHUNK), dtype=bool))
    bd = jnp.asarray(_block_diag_ones(), BF16)
    lvl = jnp.asarray(_level_table())
    for l in range(depth):
        p = {
            "g1": norm1_g[l].reshape(1, D_MODEL),
            "w_in": w_in[l].astype(BF16),
            "w_out": w_out[l].astype(BF16),
            "bd": bd,
            "a_w": jnp.where(causal, gmlp_w_s[l], 0.0).astype(BF16),
            "a_b": jnp.repeat(gmlp_b_s[l].T, HEAD_DIM, axis=1),
            "a_g": gmlp_vnorm_g[l].reshape(1, A_WIDTH),
            "q_g": jnp.tile(q_norm_g[l], B_Q_HEADS).reshape(1, B_WIDTH) * (HEAD_DIM ** -0.5),
            "k_g": jnp.tile(k_norm_g[l], B_KV_HEADS).reshape(1, B_KV_WIDTH),
            "bias": bias_tables,
            "sinks": attn_sinks[l].astype(F32),
            "lb": lower_bounds[l].reshape(1, C_WIDTH),
            "o_g": jnp.tile(hgrn_onorm_g[l], C_HEADS).reshape(1, C_WIDTH),
            "lvl": lvl,
            "g2": norm2_g[l].reshape(1, D_MODEL),
            "w_gate": w_gate[l].astype(BF16),
            "w_up": w_up[l].astype(BF16),
            "conv_w": conv_w[l],
            "conv_b": conv_b[l].reshape(1, D_FF),
            "w_down": w_down[l].astype(BF16),
        }
        x = _mixer_layer(x, p, mixer_rows)
        x = _ffn_layer(x, p, ffn_rows)
    return x
```

```python
import functools
import math

import numpy as np
import jax
import jax.numpy as jnp
from jax import lax
from jax.experimental import pallas as pl
from jax.experimental.pallas import tpu as pltpu

F32 = jnp.float32
BF16 = jnp.bfloat16

D_MODEL = 1024
HEAD_DIM = 64
CHUNK = 128
A_GROUPS = 4
A_WIDTH = 256
B_Q_HEADS = 8
B_KV_HEADS = 2
B_GROUP = 4
B_WIDTH = 512
B_KV_WIDTH = 128
REL_BUCKETS = 32
REL_MAX_DIST = 128
C_HEADS = 4
C_WIDTH = 256
IN_WIDTH = 2304
D_FF = 2816
CONV_WIDTH = 3
EPS = 1e-6
MASK_VALUE = -1e30
LOG2E = math.log2(math.e)

OFF_AU, OFF_AV, OFF_BQ, OFF_BK, OFF_BV = 0, 256, 512, 1024, 1152
OFF_CQ, OFF_CF, OFF_CI, OFF_CG = 1280, 1536, 1792, 2048

HGRN_LEVELS = (64, 32, 16, 8, 4, 2, 1)
HGRN_REF_ROW = CHUNK // 2 - 1
HGRN_SAFE_LOG_RANGE = 80.0

MIXER_ROWS = 256
FFN_ROWS = 256
VMEM_LIMIT_BYTES = 56 * 1024 * 1024

_dot = functools.partial(jnp.dot, preferred_element_type=F32)
_dot_nt = functools.partial(lax.dot_general, dimension_numbers=(((1,), (1,)), ((), ())),
                            preferred_element_type=F32)


def _split2(x):
    hi = x.astype(BF16)
    mid = (x - hi.astype(F32)).astype(BF16)
    return hi, mid


def _seg_mean_sq(x, bd):
    sq = (x * x).astype(BF16)
    w = x.shape[-1]
    outs = []
    for c0 in range(0, w, 256):
        cw = min(256, w - c0)
        outs.append(_dot(sq[:, c0:c0 + cw], bd[0:cw, 0:cw]))
    out = outs[0] if len(outs) == 1 else jnp.concatenate(outs, axis=-1)
    return out * (1.0 / HEAD_DIM)


def _sigmoid(x):
    return 1.0 / (1.0 + jnp.exp(-x))


def _gelu(x):
    return 0.5 * x * (1.0 + lax.erf(x * (1.0 / math.sqrt(2.0))))


def _rms_rows(x, g):
    ms = jnp.mean(x * x, axis=-1, keepdims=True)
    return x * lax.rsqrt(ms + EPS) * g


def _gmlp_chunk(u, vb, aw_ref, ab, lane_group):
    sv = jnp.zeros((CHUNK, A_WIDTH), F32)
    for g in range(A_GROUPS):
        r = _dot(aw_ref[g], vb)
        sv = jnp.where(lane_group == g, r, sv)
    return u * (sv + ab)


def _attn_chunk(qb, kk, vk, bias_ref, bias_idx, sink_ref):
    outs = []
    for h in range(B_Q_HEADS):
        g = h // B_GROUP
        qh = qb[:, h * HEAD_DIM:(h + 1) * HEAD_DIM]
        kh = kk[:, g * HEAD_DIM:(g + 1) * HEAD_DIM]
        vh = vk[:, g * HEAD_DIM:(g + 1) * HEAD_DIM]
        s = _dot_nt(qh, kh) + bias_ref[bias_idx, h]
        sink = sink_ref[h] * LOG2E
        m = jnp.maximum(jnp.max(s, axis=-1, keepdims=True), sink)
        p = jnp.exp2(s - m)
        den = jnp.sum(p, axis=-1, keepdims=True) + jnp.exp2(sink - m)
        o = _dot(p.astype(BF16), vh)
        outs.append(o * (1.0 / den))
    return jnp.concatenate(outs, axis=-1)


def _head_stack(xb, lane_head):
    zero = jnp.zeros_like(xb)
    return jnp.concatenate([jnp.where(lane_head == h, xb, zero) for h in range(C_HEADS)], axis=0)


def _hgrn_gates(cq, cf, prm, tri):
    log_lb, log_1m_lb, one_m_lb = prm
    qs = cq * _sigmoid(cq)
    e = jnp.exp(-jnp.abs(cf))
    r = 1.0 / (1.0 + e)
    ls = jnp.minimum(cf, 0.0) - jnp.log(1.0 + e)
    kin = one_m_lb * jnp.where(cf > 0.0, e * r, r)
    bq = log_1m_lb + ls
    mx = jnp.maximum(log_lb, bq)
    mn = jnp.minimum(log_lb, bq)
    lf = mx + jnp.log(1.0 + jnp.exp(mn - mx))
    hi, mid = _split2(lf)
    b = _dot(tri, hi) + _dot(tri, mid)
    return qs, kin, b


def _hgrn_scores_one_ref(qs, kin, b, lvl_ref, lane_head):
    d = b - b[HGRN_REF_ROW:HGRN_REF_ROW + 1, :]
    s = _dot_nt((qs * jnp.exp(d)).astype(BF16), _head_stack((kin * jnp.exp(-d)).astype(BF16), lane_head))
    return jnp.where(lvl_ref[...] >= 0, s, 0.0)


def _hgrn_scores_levels(qs, kin, b, lvl_ref, lane_head, row_id):
    lvl = lvl_ref[...]
    a = jnp.where(lvl == 0, _dot_nt(qs.astype(BF16), _head_stack(kin.astype(BF16), lane_head)), 0.0)
    for li, half in enumerate(HGRN_LEVELS):
        blk = 2 * half
        if blk >= 8:
            pieces = []
            for j in range(CHUNK // blk):
                r = j * blk + half - 1
                pieces.append(jnp.broadcast_to(b[r:r + 1, :], (blk, C_WIDTH)))
            bref = pieces[0] if len(pieces) == 1 else jnp.concatenate(pieces, axis=0)
        else:
            pos = row_id % blk
            bref = b
            for p in range(blk):
                shift = p - (half - 1)
                if shift == 0:
                    continue
                bref = jnp.where(pos == p, pltpu.roll(b, shift % CHUNK, axis=0), bref)
        e = jnp.exp(-jnp.abs(b - bref))
        s = _dot_nt((qs * e).astype(BF16), _head_stack((kin * e).astype(BF16), lane_head))
        a = jnp.where(lvl == li + 1, s, a)
    return a


def _hgrn_finish(a, qs, kin, b, ci, cg, st, og, lane_head, bd):
    total = b[CHUNK - 1:CHUNK, :]
    o = _dot(a, _head_stack(ci.astype(BF16), lane_head))
    o = o + _dot_nt((qs * jnp.exp(b)).astype(BF16), st.astype(BF16))
    kdec = (kin * jnp.exp(total - b)).astype(BF16)
    upd = _dot(ci.T.astype(BF16), kdec)
    vrow_head = lax.broadcasted_iota(jnp.int32, (C_WIDTH, C_WIDTH), 0) // HEAD_DIM
    kcol_head = lax.broadcasted_iota(jnp.int32, (C_WIDTH, C_WIDTH), 1) // HEAD_DIM
    st = st * jnp.exp(total) + jnp.where(vrow_head == kcol_head, upd, 0.0)
    o = o * lax.rsqrt(_seg_mean_sq(o, bd) + EPS) * og
    return o * (cg * _sigmoid(cg)), st


def _mixer_kernel(x_ref, g1_ref, win_ref, wout_ref, bd_ref, aw_ref, ab_ref, ag_ref, qg_ref, kg_ref,
                  bias_ref, sink_ref, lb_ref, og_ref, lvl_ref, o_ref, kv_ref, st_ref, a_ref, *, rows):
    j = pl.program_id(1)
    n_chunks = rows // CHUNK

    @pl.when(j == 0)
    def _():
        kv_ref[...] = jnp.zeros_like(kv_ref)
        st_ref[...] = jnp.zeros_like(st_ref)

    x = x_ref[0]
    h = _rms_rows(x, g1_ref[...]).astype(BF16)
    proj = _dot(h, win_ref[...])
    bd = bd_ref[...]

    u = _gelu(proj[:, OFF_AU:OFF_AU + A_WIDTH])
    v = _gelu(proj[:, OFF_AV:OFF_AV + A_WIDTH])
    v = (v * lax.rsqrt(_seg_mean_sq(v, bd) + EPS) * ag_ref[...]).astype(BF16)
    ab = ab_ref[...]
    lane_group = lax.broadcasted_iota(jnp.int32, (CHUNK, A_WIDTH), 1) // HEAD_DIM

    q = proj[:, OFF_BQ:OFF_BQ + B_WIDTH]
    q = (q * lax.rsqrt(_seg_mean_sq(q, bd) + EPS) * qg_ref[...]).astype(BF16)
    k = proj[:, OFF_BK:OFF_BK + B_KV_WIDTH]
    k = (k * lax.rsqrt(_seg_mean_sq(k, bd) + EPS) * kg_ref[...]).astype(BF16)
    vv = proj[:, OFF_BV:OFF_BV + B_KV_WIDTH].astype(BF16)
    k_prev = kv_ref[:, 0:B_KV_WIDTH]
    v_prev = kv_ref[:, B_KV_WIDTH:2 * B_KV_WIDTH]

    lb = lb_ref[...]
    prm = (jnp.log(lb), jnp.log1p(-lb), 1.0 - lb)
    tri = (lax.broadcasted_iota(jnp.int32, (CHUNK, CHUNK), 0)
           >= lax.broadcasted_iota(jnp.int32, (CHUNK, CHUNK), 1)).astype(BF16)
    lane_head = lax.broadcasted_iota(jnp.int32, (CHUNK, C_WIDTH), 1) // HEAD_DIM

    ya, yb, gates = [], [], []
    worst = jnp.zeros((1, C_WIDTH), F32)
    for c in range(n_chunks):
        sl = slice(c * CHUNK, (c + 1) * CHUNK)
        ya.append(_gmlp_chunk(u[sl], v[sl], aw_ref, ab, lane_group))
        kc, vc = k[sl], vv[sl]
        bias_idx = jnp.where(j == 0, 0, 1) if c == 0 else 1
        yb.append(_attn_chunk(q[sl], jnp.concatenate([k_prev, kc], axis=0),
                              jnp.concatenate([v_prev, vc], axis=0), bias_ref, bias_idx, sink_ref))
        k_prev, v_prev = kc, vc
        qs, kin, b = _hgrn_gates(proj[sl, OFF_CQ:OFF_CQ + C_WIDTH], proj[sl, OFF_CF:OFF_CF + C_WIDTH], prm, tri)
        gates.append((qs, kin, b))
        a_ref[c] = _hgrn_scores_one_ref(qs, kin, b, lvl_ref, lane_head).astype(BF16)
        m = b[HGRN_REF_ROW:HGRN_REF_ROW + 1, :]
        worst = jnp.maximum(worst, jnp.maximum(-m, m - b[CHUNK - 1:CHUNK, :]))
    kv_ref[:, 0:B_KV_WIDTH] = k_prev
    kv_ref[:, B_KV_WIDTH:2 * B_KV_WIDTH] = v_prev

    @pl.when(jnp.max(worst, axis=-1, keepdims=True)[0, 0] > HGRN_SAFE_LOG_RANGE)
    def _():
        row_id = lax.broadcasted_iota(jnp.int32, (CHUNK, C_WIDTH), 0)
        for c in range(n_chunks):
            qs, kin, b = gates[c]
            a_ref[c] = _hgrn_scores_levels(qs, kin, b, lvl_ref, lane_head, row_id).astype(BF16)

    st = st_ref[...]
    og = og_ref[...]
    mixed = []
    for c in range(n_chunks):
        sl = slice(c * CHUNK, (c + 1) * CHUNK)
        qs, kin, b = gates[c]
        yc, st = _hgrn_finish(a_ref[c], qs, kin, b, proj[sl, OFF_CI:OFF_CI + C_WIDTH],
                              proj[sl, OFF_CG:OFF_CG + C_WIDTH], st, og, lane_head, bd)
        mixed.append(jnp.concatenate([ya[c], yb[c], yc], axis=-1).astype(BF16))
    st_ref[...] = st
    mixed = mixed[0] if len(mixed) == 1 else jnp.concatenate(mixed, axis=0)
    o_ref[0] = x + _dot(mixed, wout_ref[...])


def _const_spec(shape):
    nd = len(shape)
    return pl.BlockSpec(shape, lambda b, j, _nd=nd: (0,) * _nd, pipeline_mode=pl.Buffered(1))


def _mixer_layer(x, p, rows):
    bsz, seq, _ = x.shape
    grid = (bsz, seq // rows)
    x_spec = pl.BlockSpec((1, rows, D_MODEL), lambda b, j: (b, j, 0))
    consts = [p["g1"], p["w_in"], p["w_out"], p["bd"], p["a_w"], p["a_b"], p["a_g"], p["q_g"], p["k_g"],
              p["bias"]]
    tail = [p["lb"], p["o_g"], p["lvl"]]
    in_specs = ([x_spec] + [_const_spec(c.shape) for c in consts]
                + [pl.BlockSpec(memory_space=pltpu.SMEM)] + [_const_spec(c.shape) for c in tail])
    return pl.pallas_call(
        functools.partial(_mixer_kernel, rows=rows),
        grid=grid,
        in_specs=in_specs,
        out_specs=x_spec,
        out_shape=jax.ShapeDtypeStruct(x.shape, x.dtype),
        scratch_shapes=[pltpu.VMEM((CHUNK, 2 * B_KV_WIDTH), BF16),
                        pltpu.VMEM((C_WIDTH, C_WIDTH), F32),
                        pltpu.VMEM((rows // CHUNK, CHUNK, C_HEADS * CHUNK), BF16)],
        compiler_params=pltpu.CompilerParams(
            dimension_semantics=("arbitrary", "arbitrary"), vmem_limit_bytes=VMEM_LIMIT_BYTES),
        name="mixer_layer",
    )(x, *consts, p["sinks"], *tail)


def _ffn_kernel(x_ref, g2_ref, wg_ref, wu_ref, cw_ref, cb_ref, wd_ref, o_ref, gate_ref, *, rows):
    j = pl.program_id(1)

    @pl.when(j == 0)
    def _():
        gate_ref[0:8, :] = jnp.zeros((8, D_FF), F32)

    x = x_ref[0]
    h = _rms_rows(x, g2_ref[...]).astype(BF16)
    gate = _dot(h, wg_ref[...])
    gate_ref[8:8 + rows, :] = gate
    up = _dot(h, wu_ref[...])
    cw = cw_ref[...]
    conv = (cb_ref[...] + cw[2:3, :] * gate + cw[1:2, :] * gate_ref[7:7 + rows, :]
            + cw[0:1, :] * gate_ref[6:6 + rows, :])
    act = (conv * _sigmoid(conv) * up).astype(BF16)
    o_ref[0] = x + _dot(act, wd_ref[...])
    gate_ref[0:8, :] = gate_ref[rows:rows + 8, :]


def _ffn_layer(x, p, rows):
    bsz, seq, _ = x.shape
    grid = (bsz, seq // rows)
    x_spec = pl.BlockSpec((1, rows, D_MODEL), lambda b, j: (b, j, 0))
    consts = [p["g2"], p["w_gate"], p["w_up"], p["conv_w"], p["conv_b"], p["w_down"]]
    return pl.pallas_call(
        functools.partial(_ffn_kernel, rows=rows),
        grid=grid,
        in_specs=[x_spec] + [_const_spec(c.shape) for c in consts],
        out_specs=x_spec,
        out_shape=jax.ShapeDtypeStruct(x.shape, x.dtype),
        scratch_shapes=[pltpu.VMEM((rows + 8, D_FF), F32)],
        compiler_params=pltpu.CompilerParams(
            dimension_semantics=("arbitrary", "arbitrary"), vmem_limit_bytes=VMEM_LIMIT_BYTES),
        name="ffn_layer",
    )(x, *consts)


def _t5_causal_bucket_table():
    qi = np.arange(CHUNK)[:, None]
    kj = np.arange(2 * CHUNK)[None, :]
    dist = qi + CHUNK - kj
    max_exact = REL_BUCKETS // 2
    n = np.maximum(dist, 0)
    nf = np.maximum(n, 1).astype(np.float64)
    large = max_exact + (np.log(nf / max_exact) / math.log(REL_MAX_DIST / max_exact)
                         * (REL_BUCKETS - max_exact)).astype(np.int32)
    large = np.minimum(large, REL_BUCKETS - 1)
    bucket = np.where(n < max_exact, n, large).astype(np.int32)
    valid = (dist >= 0) & (dist < CHUNK)
    return bucket, valid, kj >= CHUNK


def _attention_bias_tables(rel_bias):
    bucket, valid, in_cur = _t5_causal_bucket_table()
    onehot = (jnp.asarray(bucket.reshape(-1))[:, None] == jnp.arange(REL_BUCKETS)[None, :]).astype(F32)
    bias = jnp.einsum("pb,bh->hp", onehot, rel_bias.astype(F32), precision=lax.Precision.HIGHEST)
    bias = bias.reshape(B_Q_HEADS, CHUNK, 2 * CHUNK) * LOG2E
    rest = jnp.where(jnp.asarray(valid)[None], bias, MASK_VALUE)
    first = jnp.where(jnp.asarray(valid & in_cur)[None], bias, MASK_VALUE)
    return jnp.stack([first, rest])


def _level_table():
    t = np.arange(CHUNK)[:, None]
    s = np.arange(CHUNK)[None, :]
    x = np.bitwise_xor(t, s)
    msb = np.floor(np.log2(np.maximum(x, 1))).astype(np.int32)
    lvl = np.where(s == t, 0, np.where(s < t, len(HGRN_LEVELS) - msb, -1)).astype(np.int32)
    return np.tile(lvl, (1, C_HEADS))


def _block_diag_ones():
    i = np.arange(256)
    return (i[:, None] // HEAD_DIM == i[None, :] // HEAD_DIM).astype(np.float32)


def kernel(x, norm1_g, w_in, gmlp_vnorm_g, gmlp_w_s, gmlp_b_s, q_norm_g, k_norm_g, attn_sinks, rel_bias,
           hgrn_lb_logits, hgrn_onorm_g, w_out, norm2_g, w_gate, w_up, conv_w, conv_b, w_down):
    depth = w_in.shape[0]
    seq = x.shape[1]
    mixer_rows = min(MIXER_ROWS, seq)
    ffn_rows = min(FFN_ROWS, seq)
    lb_cum = jnp.cumsum(jax.nn.softmax(hgrn_lb_logits.astype(F32), axis=0), axis=0)
    lower_bounds = lb_cum - lb_cum[0]
    bias_tables = _attention_bias_tables(rel_bias)
    causal = jnp.tril(jnp.ones((CHUNK, CHUNK), dtype=bool))
    bd = jnp.asarray(_block_diag_ones(), BF16)
    lvl = jnp.asarray(_level_table())
    for l in range(depth):
        p = {
            "g1": norm1_g[l].reshape(1, D_MODEL),
            "w_in": w_in[l].astype(BF16),
            "w_out": w_out[l].astype(BF16),
            "bd": bd,
            "a_w": jnp.where(causal, gmlp_w_s[l], 0.0).astype(BF16),
            "a_b": jnp.repeat(gmlp_b_s[l].T, HEAD_DIM, axis=1),
            "a_g": gmlp_vnorm_g[l].reshape(1, A_WIDTH),
            "q_g": jnp.tile(q_norm_g[l], B_Q_HEADS).reshape(1, B_WIDTH) * (HEAD_DIM ** -0.5 * LOG2E),
            "k_g": jnp.tile(k_norm_g[l], B_KV_HEADS).reshape(1, B_KV_WIDTH),
            "bias": bias_tables,
            "sinks": attn_sinks[l].astype(F32),
            "lb": lower_bounds[l].reshape(1, C_WIDTH),
            "o_g": jnp.tile(hgrn_onorm_g[l], C_HEADS).reshape(1, C_WIDTH),
            "lvl": lvl,
            "g2": norm2_g[l].reshape(1, D_MODEL),
            "w_gate": w_gate[l].astype(BF16),
            "w_up": w_up[l].astype(BF16),
            "conv_w": conv_w[l],
            "conv_b": conv_b[l].reshape(1, D_FF),
            "w_down": w_down[l].astype(BF16),
        }
        x = _mixer_layer(x, p, mixer_rows)
        x = _ffn_layer(x, p, ffn_rows)
    return x
```

```python
import functools
import math

import numpy as np
import jax
import jax.numpy as jnp
from jax import lax
from jax.experimental import pallas as pl
from jax.experimental.pallas import tpu as pltpu

F32 = jnp.float32
BF16 = jnp.bfloat16

D_MODEL = 1024
HEAD_DIM = 64
CHUNK = 128
A_GROUPS = 4
A_WIDTH = 256
B_Q_HEADS = 8
B_KV_HEADS = 2
B_GROUP = 4
B_WIDTH = 512
B_KV_WIDTH = 128
REL_BUCKETS = 32
REL_MAX_DIST = 128
C_HEADS = 4
C_WIDTH = 256
IN_WIDTH = 2304
D_FF = 2816
CONV_WIDTH = 3
EPS = 1e-6
MASK_VALUE = -1e30
LOG2E = math.log2(math.e)

OFF_AU, OFF_AV, OFF_BQ, OFF_BK, OFF_BV = 0, 256, 512, 1024, 1152
OFF_CQ, OFF_CF, OFF_CI, OFF_CG = 1280, 1536, 1792, 2048

HGRN_LEVELS = (64, 32, 16, 8, 4, 2, 1)
HGRN_REF_ROW = CHUNK // 2 - 1
HGRN_SAFE_LOG_RANGE = 80.0

MIXER_ROWS = 512
SUB_ROWS = 256
FFN_ROWS = 256
VMEM_LIMIT_BYTES = 56 * 1024 * 1024

_dot = functools.partial(jnp.dot, preferred_element_type=F32)
_dot_nt = functools.partial(lax.dot_general, dimension_numbers=(((1,), (1,)), ((), ())),
                            preferred_element_type=F32)


def _split2(x):
    hi = x.astype(BF16)
    mid = (x - hi.astype(F32)).astype(BF16)
    return hi, mid


def _seg_mean_sq(x, bd):
    sq = (x * x).astype(BF16)
    w = x.shape[-1]
    outs = []
    for c0 in range(0, w, 256):
        cw = min(256, w - c0)
        outs.append(_dot(sq[:, c0:c0 + cw], bd[0:cw, 0:cw]))
    out = outs[0] if len(outs) == 1 else jnp.concatenate(outs, axis=-1)
    return out * (1.0 / HEAD_DIM)


def _sigmoid(x):
    return 1.0 / (1.0 + jnp.exp(-x))


def _gelu(x):
    return 0.5 * x * (1.0 + lax.erf(x * (1.0 / math.sqrt(2.0))))


def _rms_rows(x, g):
    ms = jnp.mean(x * x, axis=-1, keepdims=True)
    return x * lax.rsqrt(ms + EPS) * g


def _gmlp_chunk(u, vb, aw_ref, ab, lane_group):
    sv = jnp.zeros((CHUNK, A_WIDTH), F32)
    for g in range(A_GROUPS):
        r = _dot(aw_ref[g], vb)
        sv = jnp.where(lane_group == g, r, sv)
    return u * (sv + ab)


def _attn_chunk(qb, kk, vk, bias_ref, bias_idx, sink_ref):
    outs = []
    for h in range(B_Q_HEADS):
        g = h // B_GROUP
        qh = qb[:, h * HEAD_DIM:(h + 1) * HEAD_DIM]
        kh = kk[:, g * HEAD_DIM:(g + 1) * HEAD_DIM]
        vh = vk[:, g * HEAD_DIM:(g + 1) * HEAD_DIM]
        s = _dot_nt(qh, kh) + bias_ref[bias_idx, h]
        sink = sink_ref[h] * LOG2E
        m = jnp.maximum(jnp.max(s, axis=-1, keepdims=True), sink)
        p = jnp.exp2(s - m)
        den = jnp.sum(p, axis=-1, keepdims=True) + jnp.exp2(sink - m)
        o = _dot(p.astype(BF16), vh)
        outs.append(o * (1.0 / den))
    return jnp.concatenate(outs, axis=-1)


def _head_stack(xb, lane_head):
    zero = jnp.zeros_like(xb)
    return jnp.concatenate([jnp.where(lane_head == h, xb, zero) for h in range(C_HEADS)], axis=0)


def _hgrn_gates(cq, cf, prm, tri):
    log_lb, log_1m_lb, one_m_lb = prm
    qs = cq * _sigmoid(cq)
    e = jnp.exp(-jnp.abs(cf))
    r = 1.0 / (1.0 + e)
    ls = jnp.minimum(cf, 0.0) - jnp.log(1.0 + e)
    kin = one_m_lb * jnp.where(cf > 0.0, e * r, r)
    bq = log_1m_lb + ls
    mx = jnp.maximum(log_lb, bq)
    mn = jnp.minimum(log_lb, bq)
    lf = mx + jnp.log(1.0 + jnp.exp(mn - mx))
    hi, mid = _split2(lf)
    b = _dot(tri, hi) + _dot(tri, mid)
    return qs, kin, b


def _hgrn_scores_one_ref(qs, kin, b, lvl_ref, lane_head):
    d = b - b[HGRN_REF_ROW:HGRN_REF_ROW + 1, :]
    s = _dot_nt((qs * jnp.exp(d)).astype(BF16), _head_stack((kin * jnp.exp(-d)).astype(BF16), lane_head))
    return jnp.where(lvl_ref[...] >= 0, s, 0.0)


def _hgrn_scores_levels(qs, kin, b, lvl_ref, lane_head, row_id):
    lvl = lvl_ref[...]
    a = jnp.where(lvl == 0, _dot_nt(qs.astype(BF16), _head_stack(kin.astype(BF16), lane_head)), 0.0)
    for li, half in enumerate(HGRN_LEVELS):
        blk = 2 * half
        if blk >= 8:
            pieces = []
            for j in range(CHUNK // blk):
                r = j * blk + half - 1
                pieces.append(jnp.broadcast_to(b[r:r + 1, :], (blk, C_WIDTH)))
            bref = pieces[0] if len(pieces) == 1 else jnp.concatenate(pieces, axis=0)
        else:
            pos = row_id % blk
            bref = b
            for p in range(blk):
                shift = p - (half - 1)
                if shift == 0:
                    continue
                bref = jnp.where(pos == p, pltpu.roll(b, shift % CHUNK, axis=0), bref)
        e = jnp.exp(-jnp.abs(b - bref))
        s = _dot_nt((qs * e).astype(BF16), _head_stack((kin * e).astype(BF16), lane_head))
        a = jnp.where(lvl == li + 1, s, a)
    return a


def _hgrn_finish(a, qs, kin, b, ci, cg, st, og, lane_head, bd):
    total = b[CHUNK - 1:CHUNK, :]
    o = _dot(a, _head_stack(ci.astype(BF16), lane_head))
    o = o + _dot_nt((qs * jnp.exp(b)).astype(BF16), st.astype(BF16))
    kdec = (kin * jnp.exp(total - b)).astype(BF16)
    upd = _dot(ci.T.astype(BF16), kdec)
    vrow_head = lax.broadcasted_iota(jnp.int32, (C_WIDTH, C_WIDTH), 0) // HEAD_DIM
    kcol_head = lax.broadcasted_iota(jnp.int32, (C_WIDTH, C_WIDTH), 1) // HEAD_DIM
    st = st * jnp.exp(total) + jnp.where(vrow_head == kcol_head, upd, 0.0)
    o = o * lax.rsqrt(_seg_mean_sq(o, bd) + EPS) * og
    return o * (cg * _sigmoid(cg)), st


def _mixer_kernel(x_ref, g1_ref, win_ref, wout_ref, bd_ref, aw_ref, ab_ref, ag_ref, qg_ref, kg_ref,
                  bias_ref, sink_ref, lb_ref, og_ref, lvl_ref, o_ref, kv_ref, st_ref, *, rows):
    j = pl.program_id(1)
    n_sub = rows // SUB_ROWS
    sub_chunks = SUB_ROWS // CHUNK

    @pl.when(j == 0)
    def _():
        kv_ref[...] = jnp.zeros_like(kv_ref)
        st_ref[...] = jnp.zeros_like(st_ref)

    bd = bd_ref[...]
    ab = ab_ref[...]
    lane_group = lax.broadcasted_iota(jnp.int32, (CHUNK, A_WIDTH), 1) // HEAD_DIM
    lb = lb_ref[...]
    prm = (jnp.log(lb), jnp.log1p(-lb), 1.0 - lb)
    tri = (lax.broadcasted_iota(jnp.int32, (CHUNK, CHUNK), 0)
           >= lax.broadcasted_iota(jnp.int32, (CHUNK, CHUNK), 1)).astype(BF16)
    lane_head = lax.broadcasted_iota(jnp.int32, (CHUNK, C_WIDTH), 1) // HEAD_DIM
    og = og_ref[...]
    k_prev = kv_ref[:, 0:B_KV_WIDTH]
    v_prev = kv_ref[:, B_KV_WIDTH:2 * B_KV_WIDTH]
    st_in = st_ref[...]

    xs, yab, gates, civ, cgv, scores = [], [], [], [], [], []
    worst = jnp.zeros((1, C_WIDTH), F32)
    for sb in range(n_sub):
        x = x_ref[0, sb * SUB_ROWS:(sb + 1) * SUB_ROWS, :]
        xs.append(x)
        h = _rms_rows(x, g1_ref[...]).astype(BF16)
        proj = _dot(h, win_ref[...])
        u = _gelu(proj[:, OFF_AU:OFF_AU + A_WIDTH])
        v = _gelu(proj[:, OFF_AV:OFF_AV + A_WIDTH])
        v = (v * lax.rsqrt(_seg_mean_sq(v, bd) + EPS) * ag_ref[...]).astype(BF16)
        q = proj[:, OFF_BQ:OFF_BQ + B_WIDTH]
        q = (q * lax.rsqrt(_seg_mean_sq(q, bd) + EPS) * qg_ref[...]).astype(BF16)
        k = proj[:, OFF_BK:OFF_BK + B_KV_WIDTH]
        k = (k * lax.rsqrt(_seg_mean_sq(k, bd) + EPS) * kg_ref[...]).astype(BF16)
        vv = proj[:, OFF_BV:OFF_BV + B_KV_WIDTH].astype(BF16)
        for c in range(sub_chunks):
            sl = slice(c * CHUNK, (c + 1) * CHUNK)
            ya = _gmlp_chunk(u[sl], v[sl], aw_ref, ab, lane_group)
            kc, vc = k[sl], vv[sl]
            bias_idx = jnp.where(j == 0, 0, 1) if (sb == 0 and c == 0) else 1
            yb = _attn_chunk(q[sl], jnp.concatenate([k_prev, kc], axis=0),
                             jnp.concatenate([v_prev, vc], axis=0), bias_ref, bias_idx, sink_ref)
            k_prev, v_prev = kc, vc
            yab.append((ya, yb))
            qs, kin, b = _hgrn_gates(proj[sl, OFF_CQ:OFF_CQ + C_WIDTH], proj[sl, OFF_CF:OFF_CF + C_WIDTH],
                                     prm, tri)
            gates.append((qs, kin, b))
            civ.append(proj[sl, OFF_CI:OFF_CI + C_WIDTH])
            cgv.append(proj[sl, OFF_CG:OFF_CG + C_WIDTH])
            scores.append(_hgrn_scores_one_ref(qs, kin, b, lvl_ref, lane_head).astype(BF16))
            m = b[HGRN_REF_ROW:HGRN_REF_ROW + 1, :]
            worst = jnp.maximum(worst, jnp.maximum(-m, m - b[CHUNK - 1:CHUNK, :]))
    kv_ref[:, 0:B_KV_WIDTH] = k_prev
    kv_ref[:, B_KV_WIDTH:2 * B_KV_WIDTH] = v_prev

    def finish(score_list):
        st = st_in
        for sb in range(n_sub):
            mixed = []
            for c in range(sub_chunks):
                i = sb * sub_chunks + c
                qs, kin, b = gates[i]
                yc, st = _hgrn_finish(score_list[i], qs, kin, b, civ[i], cgv[i], st, og, lane_head, bd)
                mixed.append(jnp.concatenate([yab[i][0], yab[i][1], yc], axis=-1).astype(BF16))
            mixed = mixed[0] if len(mixed) == 1 else jnp.concatenate(mixed, axis=0)
            o_ref[0, sb * SUB_ROWS:(sb + 1) * SUB_ROWS, :] = xs[sb] + _dot(mixed, wout_ref[...])
        st_ref[...] = st

    finish(scores)

    @pl.when(jnp.max(worst, axis=-1, keepdims=True)[0, 0] > HGRN_SAFE_LOG_RANGE)
    def _():
        row_id = lax.broadcasted_iota(jnp.int32, (CHUNK, C_WIDTH), 0)
        finish([_hgrn_scores_levels(qs, kin, b, lvl_ref, lane_head, row_id).astype(BF16)
                for qs, kin, b in gates])


def _const_spec(shape):
    nd = len(shape)
    return pl.BlockSpec(shape, lambda b, j, _nd=nd: (0,) * _nd, pipeline_mode=pl.Buffered(1))


def _mixer_layer(x, p, rows):
    bsz, seq, _ = x.shape
    grid = (bsz, seq // rows)
    x_spec = pl.BlockSpec((1, rows, D_MODEL), lambda b, j: (b, j, 0))
    consts = [p["g1"], p["w_in"], p["w_out"], p["bd"], p["a_w"], p["a_b"], p["a_g"], p["q_g"], p["k_g"],
              p["bias"]]
    tail = [p["lb"], p["o_g"], p["lvl"]]
    in_specs = ([x_spec] + [_const_spec(c.shape) for c in consts]
                + [pl.BlockSpec(memory_space=pltpu.SMEM)] + [_const_spec(c.shape) for c in tail])
    return pl.pallas_call(
        functools.partial(_mixer_kernel, rows=rows),
        grid=grid,
        in_specs=in_specs,
        out_specs=x_spec,
        out_shape=jax.ShapeDtypeStruct(x.shape, x.dtype),
        scratch_shapes=[pltpu.VMEM((CHUNK, 2 * B_KV_WIDTH), BF16),
                        pltpu.VMEM((C_WIDTH, C_WIDTH), F32)],
        compiler_params=pltpu.CompilerParams(
            dimension_semantics=("arbitrary", "arbitrary"), vmem_limit_bytes=VMEM_LIMIT_BYTES),
        name="mixer_layer",
    )(x, *consts, p["sinks"], *tail)


def _ffn_kernel(x_ref, g2_ref, wg_ref, wu_ref, cw_ref, cb_ref, wd_ref, o_ref, gate_ref, *, rows):
    j = pl.program_id(1)

    @pl.when(j == 0)
    def _():
        gate_ref[0:8, :] = jnp.zeros((8, D_FF), F32)

    x = x_ref[0]
    h = _rms_rows(x, g2_ref[...]).astype(BF16)
    gate = _dot(h, wg_ref[...])
    gate_ref[8:8 + rows, :] = gate
    up = _dot(h, wu_ref[...])
    cw = cw_ref[...]
    conv = (cb_ref[...] + cw[2:3, :] * gate + cw[1:2, :] * gate_ref[7:7 + rows, :]
            + cw[0:1, :] * gate_ref[6:6 + rows, :])
    act = (conv * _sigmoid(conv) * up).astype(BF16)
    o_ref[0] = x + _dot(act, wd_ref[...])
    gate_ref[0:8, :] = gate_ref[rows:rows + 8, :]


def _ffn_layer(x, p, rows):
    bsz, seq, _ = x.shape
    grid = (bsz, seq // rows)
    x_spec = pl.BlockSpec((1, rows, D_MODEL), lambda b, j: (b, j, 0))
    consts = [p["g2"], p["w_gate"], p["w_up"], p["conv_w"], p["conv_b"], p["w_down"]]
    return pl.pallas_call(
        functools.partial(_ffn_kernel, rows=rows),
        grid=grid,
        in_specs=[x_spec] + [_const_spec(c.shape) for c in consts],
        out_specs=x_spec,
        out_shape=jax.ShapeDtypeStruct(x.shape, x.dtype),
        scratch_shapes=[pltpu.VMEM((rows + 8, D_FF), F32)],
        compiler_params=pltpu.CompilerParams(
            dimension_semantics=("arbitrary", "arbitrary"), vmem_limit_bytes=VMEM_LIMIT_BYTES),
        name="ffn_layer",
    )(x, *consts)


def _t5_causal_bucket_table():
    qi = np.arange(CHUNK)[:, None]
    kj = np.arange(2 * CHUNK)[None, :]
    dist = qi + CHUNK - kj
    max_exact = REL_BUCKETS // 2
    n = np.maximum(dist, 0)
    nf = np.maximum(n, 1).astype(np.float64)
    large = max_exact + (np.log(nf / max_exact) / math.log(REL_MAX_DIST / max_exact)
                         * (REL_BUCKETS - max_exact)).astype(np.int32)
    large = np.minimum(large, REL_BUCKETS - 1)
    bucket = np.where(n < max_exact, n, large).astype(np.int32)
    valid = (dist >= 0) & (dist < CHUNK)
    return bucket, valid, kj >= CHUNK


def _attention_bias_tables(rel_bias):
    bucket, valid, in_cur = _t5_causal_bucket_table()
    onehot = (jnp.asarray(bucket.reshape(-1))[:, None] == jnp.arange(REL_BUCKETS)[None, :]).astype(F32)
    bias = jnp.einsum("pb,bh->hp", onehot, rel_bias.astype(F32), precision=lax.Precision.HIGHEST)
    bias = bias.reshape(B_Q_HEADS, CHUNK, 2 * CHUNK) * LOG2E
    rest = jnp.where(jnp.asarray(valid)[None], bias, MASK_VALUE)
    first = jnp.where(jnp.asarray(valid & in_cur)[None], bias, MASK_VALUE)
    return jnp.stack([first, rest])


def _level_table():
    t = np.arange(CHUNK)[:, None]
    s = np.arange(CHUNK)[None, :]
    x = np.bitwise_xor(t, s)
    msb = np.floor(np.log2(np.maximum(x, 1))).astype(np.int32)
    lvl = np.where(s == t, 0, np.where(s < t, len(HGRN_LEVELS) - msb, -1)).astype(np.int32)
    return np.tile(lvl, (1, C_HEADS))


def _block_diag_ones():
    i = np.arange(256)
    return (i[:, None] // HEAD_DIM == i[None, :] // HEAD_DIM).astype(np.float32)


def kernel(x, norm1_g, w_in, gmlp_vnorm_g, gmlp_w_s, gmlp_b_s, q_norm_g, k_norm_g, attn_sinks, rel_bias,
           hgrn_lb_logits, hgrn_onorm_g, w_out, norm2_g, w_gate, w_up, conv_w, conv_b, w_down):
    depth = w_in.shape[0]
    seq = x.shape[1]
    mixer_rows = min(MIXER_ROWS, seq)
    ffn_rows = min(FFN_ROWS, seq)
    lb_cum = jnp.cumsum(jax.nn.softmax(hgrn_lb_logits.astype(F32), axis=0), axis=0)
    lower_bounds = lb_cum - lb_cum[0]
    bias_tables = _attention_bias_tables(rel_bias)
    causal = jnp.tril(jnp.ones((CHUNK, CHUNK), dtype=bool))
    bd = jnp.asarray(_block_diag_ones(), BF16)
    lvl = jnp.asarray(_level_table())
    for l in range(depth):
        p = {
            "g1": norm1_g[l].reshape(1, D_MODEL),
            "w_in": w_in[l].astype(BF16),
            "w_out": w_out[l].astype(BF16),
            "bd": bd,
            "a_w": jnp.where(causal, gmlp_w_s[l], 0.0).astype(BF16),
            "a_b": jnp.repeat(gmlp_b_s[l].T, HEAD_DIM, axis=1),
            "a_g": gmlp_vnorm_g[l].reshape(1, A_WIDTH),
            "q_g": jnp.tile(q_norm_g[l], B_Q_HEADS).reshape(1, B_WIDTH) * (HEAD_DIM ** -0.5 * LOG2E),
            "k_g": jnp.tile(k_norm_g[l], B_KV_HEADS).reshape(1, B_KV_WIDTH),
            "bias": bias_tables,
            "sinks": attn_sinks[l].astype(F32),
            "lb": lower_bounds[l].reshape(1, C_WIDTH),
            "o_g": jnp.tile(hgrn_onorm_g[l], C_HEADS).reshape(1, C_WIDTH),
            "lvl": lvl,
            "g2": norm2_g[l].reshape(1, D_MODEL),
            "w_gate": w_gate[l].astype(BF16),
            "w_up": w_up[l].astype(BF16),
            "conv_w": conv_w[l],
            "conv_b": conv_b[l].reshape(1, D_FF),
            "w_down": w_down[l].astype(BF16),
        }
        x = _mixer_layer(x, p, mixer_rows)
        x = _ffn_layer(x, p, ffn_rows)
    return x
```

```python
import functools
import math
from typing import NamedTuple

import numpy as np
import jax
import jax.numpy as jnp
from jax import lax
from jax.experimental import pallas as pl
from jax.experimental.pallas import tpu as pltpu

F32 = jnp.float32
BF16 = jnp.bfloat16

D_MODEL = 1024
HEAD_DIM = 64
CHUNK = 128
A_GROUPS = 4
A_WIDTH = 256
B_Q_HEADS = 8
B_KV_HEADS = 2
B_GROUP = 4
B_WIDTH = 512
B_KV_WIDTH = 128
REL_BUCKETS = 32
REL_MAX_DIST = 128
C_HEADS = 4
C_WIDTH = 256
IN_WIDTH = 2304
D_FF = 2816
CONV_WIDTH = 3
EPS = 1e-6
MASK_VALUE = -1e30
LOG2E = math.log2(math.e)

OFF_AU, OFF_AV, OFF_BQ, OFF_BK, OFF_BV = 0, 256, 512, 1024, 1152
OFF_CQ, OFF_CF, OFF_CI, OFF_CG = 1280, 1536, 1792, 2048

HGRN_LEVELS = (64, 32, 16, 8, 4, 2, 1)
HGRN_REF_ROW = CHUNK // 2 - 1
HGRN_SAFE_LOG_RANGE = 80.0

MIXER_ROWS = 512
SUB_ROWS = 256
FFN_ROWS = 512
VMEM_LIMIT_BYTES = 56 * 1024 * 1024

_dot = functools.partial(jnp.dot, preferred_element_type=F32)
_dot_nt = functools.partial(lax.dot_general, dimension_numbers=(((1,), (1,)), ((), ())),
                            preferred_element_type=F32)


def _split2(x):
    hi = x.astype(BF16)
    mid = (x - hi.astype(F32)).astype(BF16)
    return hi, mid


def _seg_mean_sq(x, bd):
    sq = (x * x).astype(BF16)
    w = x.shape[-1]
    outs = []
    for c0 in range(0, w, 256):
        cw = min(256, w - c0)
        outs.append(_dot(sq[:, c0:c0 + cw], bd[0:cw, 0:cw]))
    out = outs[0] if len(outs) == 1 else jnp.concatenate(outs, axis=-1)
    return out * (1.0 / HEAD_DIM)


def _sigmoid(x):
    return 1.0 / (1.0 + jnp.exp(-x))


def _gelu(x):
    return 0.5 * x * (1.0 + lax.erf(x * (1.0 / math.sqrt(2.0))))


def _rms_rows(x, g):
    ms = jnp.mean(x * x, axis=-1, keepdims=True)
    return x * lax.rsqrt(ms + EPS) * g


def _gmlp_chunk(u, vb, aw_ref, ab, lane_group):
    sv = jnp.zeros((CHUNK, A_WIDTH), F32)
    for g in range(A_GROUPS):
        r = _dot(aw_ref[g], vb)
        sv = jnp.where(lane_group == g, r, sv)
    return u * (sv + ab)


def _attn_chunk(qb, kk, vk, bias_ref, bias_idx, sink_ref):
    outs = []
    for h in range(B_Q_HEADS):
        g = h // B_GROUP
        qh = qb[:, h * HEAD_DIM:(h + 1) * HEAD_DIM]
        kh = kk[:, g * HEAD_DIM:(g + 1) * HEAD_DIM]
        vh = vk[:, g * HEAD_DIM:(g + 1) * HEAD_DIM]
        s = _dot_nt(qh, kh) + bias_ref[bias_idx, h]
        sink = sink_ref[h] * LOG2E
        m = jnp.maximum(jnp.max(s, axis=-1, keepdims=True), sink)
        p = jnp.exp2(s - m)
        den = jnp.sum(p, axis=-1, keepdims=True) + jnp.exp2(sink - m)
        o = _dot(p.astype(BF16), vh)
        outs.append(o * (1.0 / den))
    return jnp.concatenate(outs, axis=-1)


def _head_stack(xb, lane_head):
    zero = jnp.zeros_like(xb)
    return jnp.concatenate([jnp.where(lane_head == h, xb, zero) for h in range(C_HEADS)], axis=0)


def _hgrn_gates(cq, cf, prm, tri):
    log_lb, log_1m_lb, one_m_lb = prm
    qs = cq * _sigmoid(cq)
    e = jnp.exp(-jnp.abs(cf))
    r = 1.0 / (1.0 + e)
    ls = jnp.minimum(cf, 0.0) - jnp.log(1.0 + e)
    kin = one_m_lb * jnp.where(cf > 0.0, e * r, r)
    bq = log_1m_lb + ls
    mx = jnp.maximum(log_lb, bq)
    mn = jnp.minimum(log_lb, bq)
    lf = mx + jnp.log(1.0 + jnp.exp(mn - mx))
    hi, mid = _split2(lf)
    b = _dot(tri, hi) + _dot(tri, mid)
    return qs, kin, b


def _hgrn_scores_one_ref(qs, kin, b, lvl_ref, lane_head):
    d = b - b[HGRN_REF_ROW:HGRN_REF_ROW + 1, :]
    s = _dot_nt((qs * jnp.exp(d)).astype(BF16), _head_stack((kin * jnp.exp(-d)).astype(BF16), lane_head))
    return jnp.where(lvl_ref[...] >= 0, s, 0.0)


def _hgrn_scores_levels(qs, kin, b, lvl_ref, lane_head, row_id):
    lvl = lvl_ref[...]
    a = jnp.where(lvl == 0, _dot_nt(qs.astype(BF16), _head_stack(kin.astype(BF16), lane_head)), 0.0)
    for li, half in enumerate(HGRN_LEVELS):
        blk = 2 * half
        if blk >= 8:
            pieces = []
            for j in range(CHUNK // blk):
                r = j * blk + half - 1
                pieces.append(jnp.broadcast_to(b[r:r + 1, :], (blk, C_WIDTH)))
            bref = pieces[0] if len(pieces) == 1 else jnp.concatenate(pieces, axis=0)
        else:
            pos = row_id % blk
            bref = b
            for p in range(blk):
                shift = p - (half - 1)
                if shift == 0:
                    continue
                bref = jnp.where(pos == p, pltpu.roll(b, shift % CHUNK, axis=0), bref)
        e = jnp.exp(-jnp.abs(b - bref))
        s = _dot_nt((qs * e).astype(BF16), _head_stack((kin * e).astype(BF16), lane_head))
        a = jnp.where(lvl == li + 1, s, a)
    return a


def _hgrn_finish(a, qs, kin, b, ci, cg, st, og, lane_head, bd):
    total = b[CHUNK - 1:CHUNK, :]
    o = _dot(a, _head_stack(ci.astype(BF16), lane_head))
    o = o + _dot_nt((qs * jnp.exp(b)).astype(BF16), st.astype(BF16))
    kdec = (kin * jnp.exp(total - b)).astype(BF16)
    upd = _dot(ci.T.astype(BF16), kdec)
    vrow_head = lax.broadcasted_iota(jnp.int32, (C_WIDTH, C_WIDTH), 0) // HEAD_DIM
    kcol_head = lax.broadcasted_iota(jnp.int32, (C_WIDTH, C_WIDTH), 1) // HEAD_DIM
    st = st * jnp.exp(total) + jnp.where(vrow_head == kcol_head, upd, 0.0)
    o = o * lax.rsqrt(_seg_mean_sq(o, bd) + EPS) * og
    return o * (cg * _sigmoid(cg)), st


def _mixer_kernel(x_ref, g1_ref, win_ref, wout_ref, bd_ref, aw_ref, ab_ref, ag_ref, qg_ref, kg_ref,
                  bias_ref, sink_ref, lb_ref, og_ref, lvl_ref, o_ref, kv_ref, st_ref, *, rows):
    j = pl.program_id(1)
    n_sub = rows // SUB_ROWS
    sub_chunks = SUB_ROWS // CHUNK

    @pl.when(j == 0)
    def _():
        kv_ref[...] = jnp.zeros_like(kv_ref)
        st_ref[...] = jnp.zeros_like(st_ref)

    bd = bd_ref[...]
    ab = ab_ref[...]
    lane_group = lax.broadcasted_iota(jnp.int32, (CHUNK, A_WIDTH), 1) // HEAD_DIM
    lb = lb_ref[...]
    prm = (jnp.log(lb), jnp.log1p(-lb), 1.0 - lb)
    tri = (lax.broadcasted_iota(jnp.int32, (CHUNK, CHUNK), 0)
           >= lax.broadcasted_iota(jnp.int32, (CHUNK, CHUNK), 1)).astype(BF16)
    lane_head = lax.broadcasted_iota(jnp.int32, (CHUNK, C_WIDTH), 1) // HEAD_DIM
    og = og_ref[...]
    k_prev = kv_ref[:, 0:B_KV_WIDTH]
    v_prev = kv_ref[:, B_KV_WIDTH:2 * B_KV_WIDTH]
    st_in = st_ref[...]

    xs, yab, gates, civ, cgv, scores = [], [], [], [], [], []
    worst = jnp.zeros((1, C_WIDTH), F32)
    for sb in range(n_sub):
        x = x_ref[0, sb * SUB_ROWS:(sb + 1) * SUB_ROWS, :]
        xs.append(x)
        h = _rms_rows(x, g1_ref[...]).astype(BF16)
        proj = _dot(h, win_ref[...])
        u = _gelu(proj[:, OFF_AU:OFF_AU + A_WIDTH])
        v = _gelu(proj[:, OFF_AV:OFF_AV + A_WIDTH])
        v = (v * lax.rsqrt(_seg_mean_sq(v, bd) + EPS) * ag_ref[...]).astype(BF16)
        q = proj[:, OFF_BQ:OFF_BQ + B_WIDTH]
        q = (q * lax.rsqrt(_seg_mean_sq(q, bd) + EPS) * qg_ref[...]).astype(BF16)
        k = proj[:, OFF_BK:OFF_BK + B_KV_WIDTH]
        k = (k * lax.rsqrt(_seg_mean_sq(k, bd) + EPS) * kg_ref[...]).astype(BF16)
        vv = proj[:, OFF_BV:OFF_BV + B_KV_WIDTH].astype(BF16)
        for c in range(sub_chunks):
            sl = slice(c * CHUNK, (c + 1) * CHUNK)
            ya = _gmlp_chunk(u[sl], v[sl], aw_ref, ab, lane_group)
            kc, vc = k[sl], vv[sl]
            bias_idx = jnp.where(j == 0, 0, 1) if (sb == 0 and c == 0) else 1
            yb = _attn_chunk(q[sl], jnp.concatenate([k_prev, kc], axis=0),
                             jnp.concatenate([v_prev, vc], axis=0), bias_ref, bias_idx, sink_ref)
            k_prev, v_prev = kc, vc
            yab.append((ya, yb))
            qs, kin, b = _hgrn_gates(proj[sl, OFF_CQ:OFF_CQ + C_WIDTH], proj[sl, OFF_CF:OFF_CF + C_WIDTH],
                                     prm, tri)
            gates.append((qs, kin, b))
            civ.append(proj[sl, OFF_CI:OFF_CI + C_WIDTH])
            cgv.append(proj[sl, OFF_CG:OFF_CG + C_WIDTH])
            scores.append(_hgrn_scores_one_ref(qs, kin, b, lvl_ref, lane_head).astype(BF16))
            m = b[HGRN_REF_ROW:HGRN_REF_ROW + 1, :]
            worst = jnp.maximum(worst, jnp.maximum(-m, m - b[CHUNK - 1:CHUNK, :]))
    kv_ref[:, 0:B_KV_WIDTH] = k_prev
    kv_ref[:, B_KV_WIDTH:2 * B_KV_WIDTH] = v_prev

    def finish(score_list):
        st = st_in
        for sb in range(n_sub):
            mixed = []
            for c in range(sub_chunks):
                i = sb * sub_chunks + c
                qs, kin, b = gates[i]
                yc, st = _hgrn_finish(score_list[i], qs, kin, b, civ[i], cgv[i], st, og, lane_head, bd)
                mixed.append(jnp.concatenate([yab[i][0], yab[i][1], yc], axis=-1).astype(BF16))
            mixed = mixed[0] if len(mixed) == 1 else jnp.concatenate(mixed, axis=0)
            o_ref[0, sb * SUB_ROWS:(sb + 1) * SUB_ROWS, :] = xs[sb] + _dot(mixed, wout_ref[...])
        st_ref[...] = st

    finish(scores)

    @pl.when(jnp.max(worst, axis=-1, keepdims=True)[0, 0] > HGRN_SAFE_LOG_RANGE)
    def _():
        row_id = lax.broadcasted_iota(jnp.int32, (CHUNK, C_WIDTH), 0)
        finish([_hgrn_scores_levels(qs, kin, b, lvl_ref, lane_head, row_id).astype(BF16)
                for qs, kin, b in gates])


class _LayerOf(NamedTuple):
    stack: jax.Array
    layer: int


def _operand(c):
    return c.stack if isinstance(c, _LayerOf) else c


def _const_spec(c):
    if isinstance(c, _LayerOf):
        nd = c.stack.ndim - 1
        return pl.BlockSpec((None,) + c.stack.shape[1:], lambda b, j, _l=c.layer, _nd=nd: (_l,) + (0,) * _nd,
                            pipeline_mode=pl.Buffered(1))
    nd = c.ndim
    return pl.BlockSpec(c.shape, lambda b, j, _nd=nd: (0,) * _nd, pipeline_mode=pl.Buffered(1))


def _mixer_layer(x, p, rows):
    bsz, seq, _ = x.shape
    grid = (bsz, seq // rows)
    x_spec = pl.BlockSpec((1, rows, D_MODEL), lambda b, j: (b, j, 0))
    consts = [p["g1"], p["w_in"], p["w_out"], p["bd"], p["a_w"], p["a_b"], p["a_g"], p["q_g"], p["k_g"],
              p["bias"]]
    tail = [p["lb"], p["o_g"], p["lvl"]]
    in_specs = ([x_spec] + [_const_spec(c) for c in consts]
                + [pl.BlockSpec(memory_space=pltpu.SMEM)] + [_const_spec(c) for c in tail])
    return pl.pallas_call(
        functools.partial(_mixer_kernel, rows=rows),
        grid=grid,
        in_specs=in_specs,
        out_specs=x_spec,
        out_shape=jax.ShapeDtypeStruct(x.shape, x.dtype),
        scratch_shapes=[pltpu.VMEM((CHUNK, 2 * B_KV_WIDTH), BF16),
                        pltpu.VMEM((C_WIDTH, C_WIDTH), F32)],
        compiler_params=pltpu.CompilerParams(
            dimension_semantics=("arbitrary", "arbitrary"), vmem_limit_bytes=VMEM_LIMIT_BYTES),
        name="mixer_layer",
    )(x, *map(_operand, consts), p["sinks"], *tail)


def _ffn_kernel(x_ref, g2_ref, wg_ref, wu_ref, cw_ref, cb_ref, wd_ref, o_ref, gate_ref, *, rows):
    j = pl.program_id(1)

    @pl.when(j == 0)
    def _():
        gate_ref[0:8, :] = jnp.zeros((8, D_FF), F32)

    cw = cw_ref[...]
    for sb in range(rows // SUB_ROWS):
        r0 = sb * SUB_ROWS
        x = x_ref[0, r0:r0 + SUB_ROWS, :]
        h = _rms_rows(x, g2_ref[...]).astype(BF16)
        gate = _dot(h, wg_ref[...])
        gate_ref[8 + r0:8 + r0 + SUB_ROWS, :] = gate
        up = _dot(h, wu_ref[...])
        conv = (cb_ref[...] + cw[2:3, :] * gate + cw[1:2, :] * gate_ref[7 + r0:7 + r0 + SUB_ROWS, :]
                + cw[0:1, :] * gate_ref[6 + r0:6 + r0 + SUB_ROWS, :])
        act = (conv * _sigmoid(conv) * up).astype(BF16)
        o_ref[0, r0:r0 + SUB_ROWS, :] = x + _dot(act, wd_ref[...])
    gate_ref[0:8, :] = gate_ref[rows:rows + 8, :]


def _ffn_layer(x, p, rows):
    bsz, seq, _ = x.shape
    grid = (bsz, seq // rows)
    x_spec = pl.BlockSpec((1, rows, D_MODEL), lambda b, j: (b, j, 0))
    consts = [p["g2"], p["w_gate"], p["w_up"], p["conv_w"], p["conv_b"], p["w_down"]]
    return pl.pallas_call(
        functools.partial(_ffn_kernel, rows=rows),
        grid=grid,
        in_specs=[x_spec] + [_const_spec(c) for c in consts],
        out_specs=x_spec,
        out_shape=jax.ShapeDtypeStruct(x.shape, x.dtype),
        scratch_shapes=[pltpu.VMEM((rows + 8, D_FF), F32)],
        compiler_params=pltpu.CompilerParams(
            dimension_semantics=("arbitrary", "arbitrary"), vmem_limit_bytes=VMEM_LIMIT_BYTES),
        name="ffn_layer",
    )(x, *map(_operand, consts))


def _t5_causal_bucket_table():
    qi = np.arange(CHUNK)[:, None]
    kj = np.arange(2 * CHUNK)[None, :]
    dist = qi + CHUNK - kj
    max_exact = REL_BUCKETS // 2
    n = np.maximum(dist, 0)
    nf = np.maximum(n, 1).astype(np.float64)
    large = max_exact + (np.log(nf / max_exact) / math.log(REL_MAX_DIST / max_exact)
                         * (REL_BUCKETS - max_exact)).astype(np.int32)
    large = np.minimum(large, REL_BUCKETS - 1)
    bucket = np.where(n < max_exact, n, large).astype(np.int32)
    valid = (dist >= 0) & (dist < CHUNK)
    return bucket, valid, kj >= CHUNK


def _attention_bias_tables(rel_bias):
    bucket, valid, in_cur = _t5_causal_bucket_table()
    onehot = (jnp.asarray(bucket.reshape(-1))[:, None] == jnp.arange(REL_BUCKETS)[None, :]).astype(F32)
    bias = jnp.einsum("pb,bh->hp", onehot, rel_bias.astype(F32), precision=lax.Precision.HIGHEST)
    bias = bias.reshape(B_Q_HEADS, CHUNK, 2 * CHUNK) * LOG2E
    rest = jnp.where(jnp.asarray(valid)[None], bias, MASK_VALUE)
    first = jnp.where(jnp.asarray(valid & in_cur)[None], bias, MASK_VALUE)
    return jnp.stack([first, rest])


def _level_table():
    t = np.arange(CHUNK)[:, None]
    s = np.arange(CHUNK)[None, :]
    x = np.bitwise_xor(t, s)
    msb = np.floor(np.log2(np.maximum(x, 1))).astype(np.int32)
    lvl = np.where(s == t, 0, np.where(s < t, len(HGRN_LEVELS) - msb, -1)).astype(np.int32)
    return np.tile(lvl, (1, C_HEADS))


def _block_diag_ones():
    i = np.arange(256)
    return (i[:, None] // HEAD_DIM == i[None, :] // HEAD_DIM).astype(np.float32)


def kernel(x, norm1_g, w_in, gmlp_vnorm_g, gmlp_w_s, gmlp_b_s, q_norm_g, k_norm_g, attn_sinks, rel_bias,
           hgrn_lb_logits, hgrn_onorm_g, w_out, norm2_g, w_gate, w_up, conv_w, conv_b, w_down):
    depth = w_in.shape[0]
    seq = x.shape[1]
    mixer_rows = min(MIXER_ROWS, seq)
    ffn_rows = min(FFN_ROWS, seq)
    lb_cum = jnp.cumsum(jax.nn.softmax(hgrn_lb_logits.astype(F32), axis=0), axis=0)
    lower_bounds = lb_cum - lb_cum[0]
    bias_tables = _attention_bias_tables(rel_bias)
    causal = jnp.tril(jnp.ones((CHUNK, CHUNK), dtype=bool))
    bd = jnp.asarray(_block_diag_ones(), BF16)
    lvl = jnp.asarray(_level_table())
    w_in, w_out, w_gate, w_up, w_down = (w.astype(BF16) for w in (w_in, w_out, w_gate, w_up, w_down))
    for l in range(depth):
        p = {
            "g1": norm1_g[l].reshape(1, D_MODEL),
            "w_in": _LayerOf(w_in, l),
            "w_out": _LayerOf(w_out, l),
            "bd": bd,
            "a_w": jnp.where(causal, gmlp_w_s[l], 0.0).astype(BF16),
            "a_b": jnp.repeat(gmlp_b_s[l].T, HEAD_DIM, axis=1),
            "a_g": gmlp_vnorm_g[l].reshape(1, A_WIDTH),
            "q_g": jnp.tile(q_norm_g[l], B_Q_HEADS).reshape(1, B_WIDTH) * (HEAD_DIM ** -0.5 * LOG2E),
            "k_g": jnp.tile(k_norm_g[l], B_KV_HEADS).reshape(1, B_KV_WIDTH),
            "bias": bias_tables,
            "sinks": attn_sinks[l].astype(F32),
            "lb": lower_bounds[l].reshape(1, C_WIDTH),
            "o_g": jnp.tile(hgrn_onorm_g[l], C_HEADS).reshape(1, C_WIDTH),
            "lvl": lvl,
            "g2": norm2_g[l].reshape(1, D_MODEL),
            "w_gate": _LayerOf(w_gate, l),
            "w_up": _LayerOf(w_up, l),
            "conv_w": conv_w[l],
            "conv_b": conv_b[l].reshape(1, D_FF),
            "w_down": _LayerOf(w_down, l),
        }
        x = _mixer_layer(x, p, mixer_rows)
        x = _ffn_layer(x, p, ffn_rows)
    return x
```

```python
import collections
import functools
import math
from typing import NamedTuple

import numpy as np
import jax
import jax.numpy as jnp
from jax import lax
from jax.experimental import pallas as pl
from jax.experimental.pallas import tpu as pltpu

F32 = jnp.float32
BF16 = jnp.bfloat16

D_MODEL = 1024
HEAD_DIM = 64
CHUNK = 128
A_GROUPS = 4
A_WIDTH = 256
B_Q_HEADS = 8
B_KV_HEADS = 2
B_GROUP = 4
B_WIDTH = 512
B_KV_WIDTH = 128
REL_BUCKETS = 32
REL_MAX_DIST = 128
C_HEADS = 4
C_WIDTH = 256
IN_WIDTH = 2304
D_FF = 2816
CONV_WIDTH = 3
EPS = 1e-6
MASK_VALUE = -1e30
LOG2E = math.log2(math.e)

PROJ_TILE = 256
PROJ_TILES = IN_WIDTH // PROJ_TILE
TILE_AU, TILE_AV, TILE_BQ, TILE_BKV, TILE_CQ, TILE_CF, TILE_CI, TILE_CG = 0, 1, 2, 4, 5, 6, 7, 8

HGRN_LEVELS = (64, 32, 16, 8, 4, 2, 1)
HGRN_REF_ROW = CHUNK // 2 - 1
HGRN_SAFE_LOG_RANGE = 80.0

MIXER_ROWS = 1024
SUB_ROWS = 256
FFN_ROWS = 512
VMEM_LIMIT_BYTES = 56 * 1024 * 1024

_dot = functools.partial(jnp.dot, preferred_element_type=F32)
_dot_nt = functools.partial(lax.dot_general, dimension_numbers=(((1,), (1,)), ((), ())),
                            preferred_element_type=F32)


def _split2(x):
    hi = x.astype(BF16)
    mid = (x - hi.astype(F32)).astype(BF16)
    return hi, mid


def _seg_mean_sq(x, bd):
    sq = (x * x).astype(BF16)
    w = x.shape[-1]
    outs = []
    for c0 in range(0, w, 256):
        cw = min(256, w - c0)
        outs.append(_dot(sq[:, c0:c0 + cw], bd[0:cw, 0:cw]))
    out = outs[0] if len(outs) == 1 else jnp.concatenate(outs, axis=-1)
    return out * (1.0 / HEAD_DIM)


def _sigmoid(x):
    return 1.0 / (1.0 + jnp.exp(-x))


def _gelu(x):
    return 0.5 * x * (1.0 + lax.erf(x * (1.0 / math.sqrt(2.0))))


def _rms_rows(x, g):
    ms = jnp.mean(x * x, axis=-1, keepdims=True)
    return x * lax.rsqrt(ms + EPS) * g


def _gmlp_chunk(u, vb, aw_ref, ab, lane_group):
    r = _dot(aw_ref[...], vb)
    sv = r[0:CHUNK]
    for g in range(1, A_GROUPS):
        sv = jnp.where(lane_group == g, r[g * CHUNK:(g + 1) * CHUNK], sv)
    return u * (sv + ab)


def _attn_chunk(qb, kk, vk, bias_ref, bias_idx, sink_ref, prev_side):
    outs = []
    for g in range(B_KV_HEADS):
        heads = range(g * B_GROUP, (g + 1) * B_GROUP)
        q_stack = jnp.concatenate([qb[:, h * HEAD_DIM:(h + 1) * HEAD_DIM] for h in heads], axis=0)
        s2 = _dot_nt(q_stack, kk[:, g * HEAD_DIM:(g + 1) * HEAD_DIM])
        probs, dens = [], []
        for i, h in enumerate(heads):
            rows = slice(i * CHUNK, (i + 1) * CHUNK)
            s = jnp.where(prev_side, s2[rows, 0:CHUNK], s2[rows, CHUNK:2 * CHUNK]) + bias_ref[bias_idx, h]
            sink = sink_ref[h] * LOG2E
            m = jnp.maximum(jnp.max(s, axis=-1, keepdims=True), sink)
            p = jnp.exp2(s - m)
            dens.append(jnp.sum(p, axis=-1, keepdims=True) + jnp.exp2(sink - m))
            pb = p.astype(BF16)
            zero = jnp.zeros_like(pb)
            probs.append(jnp.concatenate([jnp.where(prev_side, pb, zero), jnp.where(prev_side, zero, pb)], axis=1))
        o = _dot(jnp.concatenate(probs, axis=0), vk[:, g * HEAD_DIM:(g + 1) * HEAD_DIM])
        for i in range(B_GROUP):
            outs.append(o[i * CHUNK:(i + 1) * CHUNK] * (1.0 / dens[i]))
    return jnp.concatenate(outs, axis=-1)


def _head_stack(xb, lane_head):
    zero = jnp.zeros_like(xb)
    return jnp.concatenate([jnp.where(lane_head == h, xb, zero) for h in range(C_HEADS)], axis=0)


def _hgrn_gates(cq, cf, prm, tri):
    log_lb, log_1m_lb, one_m_lb = prm
    qs = cq * _sigmoid(cq)
    e = jnp.exp(-jnp.abs(cf))
    r = 1.0 / (1.0 + e)
    ls = jnp.minimum(cf, 0.0) - jnp.log(1.0 + e)
    kin = one_m_lb * jnp.where(cf > 0.0, e * r, r)
    bq = log_1m_lb + ls
    mx = jnp.maximum(log_lb, bq)
    mn = jnp.minimum(log_lb, bq)
    lf = mx + jnp.log(1.0 + jnp.exp(mn - mx))
    hi, mid = _split2(lf)
    b = _dot(tri, hi) + _dot(tri, mid)
    return qs, kin, b


def _hgrn_scores_one_ref(qs, kin, b, lvl_ref, lane_head):
    d = b - b[HGRN_REF_ROW:HGRN_REF_ROW + 1, :]
    s = _dot_nt((qs * jnp.exp(d)).astype(BF16), _head_stack((kin * jnp.exp(-d)).astype(BF16), lane_head))
    return jnp.where(lvl_ref[...] >= 0, s, 0.0)


def _hgrn_scores_levels(qs, kin, b, lvl_ref, lane_head, row_id):
    lvl = lvl_ref[...]
    a = jnp.where(lvl == 0, _dot_nt(qs.astype(BF16), _head_stack(kin.astype(BF16), lane_head)), 0.0)
    for li, half in enumerate(HGRN_LEVELS):
        blk = 2 * half
        if blk >= 8:
            pieces = []
            for j in range(CHUNK // blk):
                r = j * blk + half - 1
                pieces.append(jnp.broadcast_to(b[r:r + 1, :], (blk, C_WIDTH)))
            bref = pieces[0] if len(pieces) == 1 else jnp.concatenate(pieces, axis=0)
        else:
            pos = row_id % blk
            bref = b
            for p in range(blk):
                shift = p - (half - 1)
                if shift == 0:
                    continue
                bref = jnp.where(pos == p, pltpu.roll(b, shift % CHUNK, axis=0), bref)
        e = jnp.exp(-jnp.abs(b - bref))
        s = _dot_nt((qs * e).astype(BF16), _head_stack((kin * e).astype(BF16), lane_head))
        a = jnp.where(lvl == li + 1, s, a)
    return a


def _hgrn_finish(a, qs, kin, b, ci, cg, st, og, lane_head, bd):
    total = b[CHUNK - 1:CHUNK, :]
    o = _dot(a, _head_stack(ci.astype(BF16), lane_head))
    o = o + _dot_nt((qs * jnp.exp(b)).astype(BF16), st.astype(BF16))
    kdec = (kin * jnp.exp(total - b)).astype(BF16)
    upd = _dot(ci.T.astype(BF16), kdec)
    vrow_head = lax.broadcasted_iota(jnp.int32, (C_WIDTH, C_WIDTH), 0) // HEAD_DIM
    kcol_head = lax.broadcasted_iota(jnp.int32, (C_WIDTH, C_WIDTH), 1) // HEAD_DIM
    st = st * jnp.exp(total) + jnp.where(vrow_head == kcol_head, upd, 0.0)
    o = o * lax.rsqrt(_seg_mean_sq(o, bd) + EPS) * og
    return o * (cg * _sigmoid(cg)), st


def _mixer_kernel(x_ref, g1_ref, win_ref, wout_ref, bd_ref, aw_ref, ab_ref, ag_ref, qg_ref, kg_ref,
                  bias_ref, sink_ref, lb_ref, og_ref, lvl_ref, o_ref, kv_ref, st_ref, *, rows):
    j = pl.program_id(1)
    n_sub = rows // SUB_ROWS
    sub_chunks = SUB_ROWS // CHUNK

    @pl.when(j == 0)
    def _():
        kv_ref[...] = jnp.zeros_like(kv_ref)
        st_ref[...] = jnp.zeros_like(st_ref)

    bd = bd_ref[...]
    ab = ab_ref[...]
    lane_group = lax.broadcasted_iota(jnp.int32, (CHUNK, A_WIDTH), 1) // HEAD_DIM
    prev_side = (lax.broadcasted_iota(jnp.int32, (CHUNK, CHUNK), 1)
                 > lax.broadcasted_iota(jnp.int32, (CHUNK, CHUNK), 0))
    lb = lb_ref[...]
    prm = (jnp.log(lb), jnp.log1p(-lb), 1.0 - lb)
    tri = (lax.broadcasted_iota(jnp.int32, (CHUNK, CHUNK), 0)
           >= lax.broadcasted_iota(jnp.int32, (CHUNK, CHUNK), 1)).astype(BF16)
    lane_head = lax.broadcasted_iota(jnp.int32, (CHUNK, C_WIDTH), 1) // HEAD_DIM
    og = og_ref[...]
    st_in = st_ref[...]

    n_chunks = n_sub * sub_chunks
    xs, hs = [None] * n_sub, [None] * n_sub
    tiles = [[None] * PROJ_TILES for _ in range(n_sub)]
    ya, yb, gates, scores = ([None] * n_chunks for _ in range(4))
    carry = {"k": kv_ref[:, 0:B_KV_WIDTH], "v": kv_ref[:, B_KV_WIDTH:2 * B_KV_WIDTH],
             "worst": jnp.zeros((1, C_WIDTH), F32)}
    pending = collections.deque()

    def emit_pending(cost, cost_left):
        for _ in range(-(-len(pending) * cost // cost_left)):
            pending.popleft()()

    def load_rows(sb):
        xs[sb] = x_ref[0, sb * SUB_ROWS:(sb + 1) * SUB_ROWS, :]
        hs[sb] = _rms_rows(xs[sb], g1_ref[...]).astype(BF16)

    def project_tile(sb, t):
        tiles[sb][t] = _dot(hs[sb], win_ref[:, t * PROJ_TILE:(t + 1) * PROJ_TILE])

    def tile_rows(sb, t, c):
        return tiles[sb][t][c * CHUNK:(c + 1) * CHUNK]

    def mixer_pieces(sb):
        local = {}

        def prep_a():
            local["u"] = _gelu(tiles[sb][TILE_AU])
            v = _gelu(tiles[sb][TILE_AV])
            local["v"] = (v * lax.rsqrt(_seg_mean_sq(v, bd) + EPS) * ag_ref[...]).astype(BF16)

        def prep_q():
            q = jnp.concatenate([tiles[sb][TILE_BQ], tiles[sb][TILE_BQ + 1]], axis=-1)
            local["q"] = (q * lax.rsqrt(_seg_mean_sq(q, bd) + EPS) * qg_ref[...]).astype(BF16)

        def prep_kv():
            k = tiles[sb][TILE_BKV][:, 0:B_KV_WIDTH]
            local["k"] = (k * lax.rsqrt(_seg_mean_sq(k, bd) + EPS) * kg_ref[...]).astype(BF16)
            local["vv"] = tiles[sb][TILE_BKV][:, B_KV_WIDTH:2 * B_KV_WIDTH].astype(BF16)

        pieces = [(2, prep_a), (1, prep_q), (1, prep_kv)]
        for c in range(sub_chunks):
            i = sb * sub_chunks + c
            sl = slice(c * CHUNK, (c + 1) * CHUNK)

            def gmlp(i=i, sl=sl):
                ya[i] = _gmlp_chunk(local["u"][sl], local["v"][sl], aw_ref, ab, lane_group)

            def attn(i=i, sl=sl):
                kc, vc = local["k"][sl], local["vv"][sl]
                bias_idx = jnp.where(j == 0, 0, 1) if i == 0 else 1
                yb[i] = _attn_chunk(local["q"][sl], jnp.concatenate([carry["k"], kc], axis=0),
                                    jnp.concatenate([carry["v"], vc], axis=0), bias_ref, bias_idx, sink_ref,
                                    prev_side)
                carry["k"], carry["v"] = kc, vc

            def hgrn(i=i, c=c):
                qs, kin, b = _hgrn_gates(tile_rows(sb, TILE_CQ, c), tile_rows(sb, TILE_CF, c), prm, tri)
                gates[i] = (qs, kin, b)
                scores[i] = _hgrn_scores_one_ref(qs, kin, b, lvl_ref, lane_head).astype(BF16)
                m = b[HGRN_REF_ROW:HGRN_REF_ROW + 1, :]
                carry["worst"] = jnp.maximum(carry["worst"], jnp.maximum(-m, m - b[CHUNK - 1:CHUNK, :]))

            pieces += [(1, gmlp), (4, attn), (2, hgrn)]
        return pieces

    def finish_pieces(sb, score_list, state):
        mixed = []

        def chunk_out(c):
            i = sb * sub_chunks + c
            qs, kin, b = gates[i]
            yc, state["st"] = _hgrn_finish(score_list[i], qs, kin, b, tile_rows(sb, TILE_CI, c),
                                           tile_rows(sb, TILE_CG, c), state["st"], og, lane_head, bd)
            mixed.append(jnp.concatenate([ya[i], yb[i], yc], axis=-1).astype(BF16))

        def out_tile(t):
            m = mixed[0] if len(mixed) == 1 else jnp.concatenate(mixed, axis=0)
            cols = slice(t * PROJ_TILE, (t + 1) * PROJ_TILE)
            o_ref[0, sb * SUB_ROWS:(sb + 1) * SUB_ROWS, cols] = xs[sb][:, cols] + _dot(m, wout_ref[:, cols])

        return ([functools.partial(chunk_out, c) for c in range(sub_chunks)]
                + [functools.partial(out_tile, t) for t in range(D_MODEL // PROJ_TILE)])

    state = {"st": st_in}
    load_rows(0)
    for t in range(PROJ_TILES):
        project_tile(0, t)
    for sb in range(n_sub):
        if sb + 1 < n_sub:
            pending.append(functools.partial(load_rows, sb + 1))
            pending.extend(functools.partial(project_tile, sb + 1, t) for t in range(PROJ_TILES))
        pieces = mixer_pieces(sb)
        cost_left = sum(cost for cost, _ in pieces)
        for cost, piece in pieces:
            piece()
            emit_pending(cost, cost_left)
            cost_left -= cost
        pending.extend(finish_pieces(sb, scores, state))
    while pending:
        pending.popleft()()
    st_ref[...] = state["st"]
    kv_ref[:, 0:B_KV_WIDTH] = carry["k"]
    kv_ref[:, B_KV_WIDTH:2 * B_KV_WIDTH] = carry["v"]

    @pl.when(jnp.max(carry["worst"], axis=-1, keepdims=True)[0, 0] > HGRN_SAFE_LOG_RANGE)
    def _():
        row_id = lax.broadcasted_iota(jnp.int32, (CHUNK, C_WIDTH), 0)
        safe = [_hgrn_scores_levels(qs, kin, b, lvl_ref, lane_head, row_id).astype(BF16) for qs, kin, b in gates]
        redo = {"st": st_in}
        for sb in range(n_sub):
            for piece in finish_pieces(sb, safe, redo):
                piece()
        st_ref[...] = redo["st"]


class _LayerOf(NamedTuple):
    stack: jax.Array
    layer: int


def _operand(c):
    return c.stack if isinstance(c, _LayerOf) else c


def _const_spec(c):
    if isinstance(c, _LayerOf):
        nd = c.stack.ndim - 1
        return pl.BlockSpec((None,) + c.stack.shape[1:], lambda b, j, _l=c.layer, _nd=nd: (_l,) + (0,) * _nd,
                            pipeline_mode=pl.Buffered(1))
    nd = c.ndim
    return pl.BlockSpec(c.shape, lambda b, j, _nd=nd: (0,) * _nd, pipeline_mode=pl.Buffered(1))


def _mixer_layer(x, p, rows):
    bsz, seq, _ = x.shape
    grid = (bsz, seq // rows)
    x_spec = pl.BlockSpec((1, rows, D_MODEL), lambda b, j: (b, j, 0))
    consts = [p["g1"], p["w_in"], p["w_out"], p["bd"], p["a_w"], p["a_b"], p["a_g"], p["q_g"], p["k_g"],
              p["bias"]]
    tail = [p["lb"], p["o_g"], p["lvl"]]
    in_specs = ([x_spec] + [_const_spec(c) for c in consts]
                + [pl.BlockSpec(memory_space=pltpu.SMEM)] + [_const_spec(c) for c in tail])
    return pl.pallas_call(
        functools.partial(_mixer_kernel, rows=rows),
        grid=grid,
        in_specs=in_specs,
        out_specs=x_spec,
        out_shape=jax.ShapeDtypeStruct(x.shape, x.dtype),
        scratch_shapes=[pltpu.VMEM((CHUNK, 2 * B_KV_WIDTH), BF16),
                        pltpu.VMEM((C_WIDTH, C_WIDTH), F32)],
        compiler_params=pltpu.CompilerParams(
            dimension_semantics=("arbitrary", "arbitrary"), vmem_limit_bytes=VMEM_LIMIT_BYTES),
        name="mixer_layer",
    )(x, *map(_operand, consts), p["sinks"], *tail)


def _ffn_kernel(x_ref, g2_ref, wg_ref, wu_ref, cw_ref, cb_ref, wd_ref, o_ref, gate_ref, *, rows):
    j = pl.program_id(1)

    @pl.when(j == 0)
    def _():
        gate_ref[0:8, :] = jnp.zeros((8, D_FF), F32)

    cw = cw_ref[...]
    for sb in range(rows // SUB_ROWS):
        r0 = sb * SUB_ROWS
        x = x_ref[0, r0:r0 + SUB_ROWS, :]
        h = _rms_rows(x, g2_ref[...]).astype(BF16)
        gate = _dot(h, wg_ref[...])
        gate_ref[8 + r0:8 + r0 + SUB_ROWS, :] = gate
        up = _dot(h, wu_ref[...])
        conv = (cb_ref[...] + cw[2:3, :] * gate + cw[1:2, :] * gate_ref[7 + r0:7 + r0 + SUB_ROWS, :]
                + cw[0:1, :] * gate_ref[6 + r0:6 + r0 + SUB_ROWS, :])
        act = (conv * _sigmoid(conv) * up).astype(BF16)
        o_ref[0, r0:r0 + SUB_ROWS, :] = x + _dot(act, wd_ref[...])
    gate_ref[0:8, :] = gate_ref[rows:rows + 8, :]


def _ffn_layer(x, p, rows):
    bsz, seq, _ = x.shape
    grid = (bsz, seq // rows)
    x_spec = pl.BlockSpec((1, rows, D_MODEL), lambda b, j: (b, j, 0))
    consts = [p["g2"], p["w_gate"], p["w_up"], p["conv_w"], p["conv_b"], p["w_down"]]
    return pl.pallas_call(
        functools.partial(_ffn_kernel, rows=rows),
        grid=grid,
        in_specs=[x_spec] + [_const_spec(c) for c in consts],
        out_specs=x_spec,
        out_shape=jax.ShapeDtypeStruct(x.shape, x.dtype),
        scratch_shapes=[pltpu.VMEM((rows + 8, D_FF), F32)],
        compiler_params=pltpu.CompilerParams(
            dimension_semantics=("arbitrary", "arbitrary"), vmem_limit_bytes=VMEM_LIMIT_BYTES),
        name="ffn_layer",
    )(x, *map(_operand, consts))


def _t5_causal_bucket_table():
    qi = np.arange(CHUNK)[:, None]
    kj = np.arange(CHUNK)[None, :]
    prev_side = kj > qi
    dist = np.where(prev_side, qi + CHUNK - kj, qi - kj)
    max_exact = REL_BUCKETS // 2
    nf = np.maximum(dist, 1).astype(np.float64)
    large = max_exact + (np.log(nf / max_exact) / math.log(REL_MAX_DIST / max_exact)
                         * (REL_BUCKETS - max_exact)).astype(np.int32)
    large = np.minimum(large, REL_BUCKETS - 1)
    bucket = np.where(dist < max_exact, dist, large).astype(np.int32)
    return bucket, prev_side


def _attention_bias_tables(rel_bias):
    bucket, prev_side = _t5_causal_bucket_table()
    onehot = (jnp.asarray(bucket.reshape(-1))[:, None] == jnp.arange(REL_BUCKETS)[None, :]).astype(F32)
    bias = jnp.einsum("pb,bh->hp", onehot, rel_bias.astype(F32), precision=lax.Precision.HIGHEST)
    bias = bias.reshape(B_Q_HEADS, CHUNK, CHUNK) * LOG2E
    first = jnp.where(jnp.asarray(prev_side)[None], MASK_VALUE, bias)
    return jnp.stack([first, bias])


def _level_table():
    t = np.arange(CHUNK)[:, None]
    s = np.arange(CHUNK)[None, :]
    x = np.bitwise_xor(t, s)
    msb = np.floor(np.log2(np.maximum(x, 1))).astype(np.int32)
    lvl = np.where(s == t, 0, np.where(s < t, len(HGRN_LEVELS) - msb, -1)).astype(np.int32)
    return np.tile(lvl, (1, C_HEADS))


def _block_diag_ones():
    i = np.arange(256)
    return (i[:, None] // HEAD_DIM == i[None, :] // HEAD_DIM).astype(np.float32)


def kernel(x, norm1_g, w_in, gmlp_vnorm_g, gmlp_w_s, gmlp_b_s, q_norm_g, k_norm_g, attn_sinks, rel_bias,
           hgrn_lb_logits, hgrn_onorm_g, w_out, norm2_g, w_gate, w_up, conv_w, conv_b, w_down):
    depth = w_in.shape[0]
    seq = x.shape[1]
    mixer_rows = min(MIXER_ROWS, seq)
    ffn_rows = min(FFN_ROWS, seq)
    lb_cum = jnp.cumsum(jax.nn.softmax(hgrn_lb_logits.astype(F32), axis=0), axis=0)
    lower_bounds = lb_cum - lb_cum[0]
    bias_tables = _attention_bias_tables(rel_bias)
    causal = jnp.tril(jnp.ones((CHUNK, CHUNK), dtype=bool))
    bd = jnp.asarray(_block_diag_ones(), BF16)
    lvl = jnp.asarray(_level_table())
    w_in, w_out, w_gate, w_up, w_down = (w.astype(BF16) for w in (w_in, w_out, w_gate, w_up, w_down))
    for l in range(depth):
        p = {
            "g1": norm1_g[l].reshape(1, D_MODEL),
            "w_in": _LayerOf(w_in, l),
            "w_out": _LayerOf(w_out, l),
            "bd": bd,
            "a_w": jnp.where(causal, gmlp_w_s[l], 0.0).astype(BF16).reshape(A_GROUPS * CHUNK, CHUNK),
            "a_b": jnp.repeat(gmlp_b_s[l].T, HEAD_DIM, axis=1),
            "a_g": gmlp_vnorm_g[l].reshape(1, A_WIDTH),
            "q_g": jnp.tile(q_norm_g[l], B_Q_HEADS).reshape(1, B_WIDTH) * (HEAD_DIM ** -0.5 * LOG2E),
            "k_g": jnp.tile(k_norm_g[l], B_KV_HEADS).reshape(1, B_KV_WIDTH),
            "bias": bias_tables,
            "sinks": attn_sinks[l].astype(F32),
            "lb": lower_bounds[l].reshape(1, C_WIDTH),
            "o_g": jnp.tile(hgrn_onorm_g[l], C_HEADS).reshape(1, C_WIDTH),
            "lvl": lvl,
            "g2": norm2_g[l].reshape(1, D_MODEL),
            "w_gate": _LayerOf(w_gate, l),
            "w_up": _LayerOf(w_up, l),
            "conv_w": conv_w[l],
            "conv_b": conv_b[l].reshape(1, D_FF),
            "w_down": _LayerOf(w_down, l),
        }
        x = _mixer_layer(x, p, mixer_rows)
        x = _ffn_layer(x, p, ffn_rows)
    return x
```

```python
import collections
import functools
import math
from typing import NamedTuple

import numpy as np
import jax
import jax.numpy as jnp
from jax import lax
from jax.experimental import pallas as pl
from jax.experimental.pallas import tpu as pltpu

F32 = jnp.float32
BF16 = jnp.bfloat16

D_MODEL = 1024
HEAD_DIM = 64
CHUNK = 128
A_GROUPS = 4
A_WIDTH = 256
B_Q_HEADS = 8
B_KV_HEADS = 2
B_GROUP = 4
B_WIDTH = 512
B_KV_WIDTH = 128
REL_BUCKETS = 32
REL_MAX_DIST = 128
C_HEADS = 4
C_WIDTH = 256
IN_WIDTH = 2304
D_FF = 2816
CONV_WIDTH = 3
EPS = 1e-6
MASK_VALUE = -1e30
LOG2E = math.log2(math.e)

PROJ_TILE = 256
PROJ_TILES = IN_WIDTH // PROJ_TILE
TILE_AU, TILE_AV, TILE_BQ, TILE_BKV, TILE_CQ, TILE_CF, TILE_CI, TILE_CG = 0, 1, 2, 4, 5, 6, 7, 8

HGRN_LEVELS = (64, 32, 16, 8, 4, 2, 1)
HGRN_REF_ROW = CHUNK // 2 - 1
HGRN_SAFE_LOG_RANGE = 80.0

LAYER_ROWS = 512
SUB_ROWS = 256
VMEM_LIMIT_BYTES = 56 * 1024 * 1024

_dot = functools.partial(jnp.dot, preferred_element_type=F32)
_dot_nt = functools.partial(lax.dot_general, dimension_numbers=(((1,), (1,)), ((), ())),
                            preferred_element_type=F32)


def _split2(x):
    hi = x.astype(BF16)
    mid = (x - hi.astype(F32)).astype(BF16)
    return hi, mid


def _seg_mean_sq(x, bd):
    sq = (x * x).astype(BF16)
    w = x.shape[-1]
    outs = []
    for c0 in range(0, w, 256):
        cw = min(256, w - c0)
        outs.append(_dot(sq[:, c0:c0 + cw], bd[0:cw, 0:cw]))
    out = outs[0] if len(outs) == 1 else jnp.concatenate(outs, axis=-1)
    return out * (1.0 / HEAD_DIM)


def _sigmoid(x):
    return 1.0 / (1.0 + jnp.exp(-x))


def _gelu(x):
    return 0.5 * x * (1.0 + lax.erf(x * (1.0 / math.sqrt(2.0))))


def _rms_rows(x, g):
    ms = jnp.mean(x * x, axis=-1, keepdims=True)
    return x * lax.rsqrt(ms + EPS) * g


def _gmlp_chunk(u, vb, aw_ref, ab, lane_group):
    r = _dot(aw_ref[...], vb)
    sv = r[0:CHUNK]
    for g in range(1, A_GROUPS):
        sv = jnp.where(lane_group == g, r[g * CHUNK:(g + 1) * CHUNK], sv)
    return u * (sv + ab)


def _attn_chunk(qb, kk, vk, bias_ref, bias_idx, sink_ref, prev_side):
    outs = []
    for g in range(B_KV_HEADS):
        heads = range(g * B_GROUP, (g + 1) * B_GROUP)
        q_stack = jnp.concatenate([qb[:, h * HEAD_DIM:(h + 1) * HEAD_DIM] for h in heads], axis=0)
        s2 = _dot_nt(q_stack, kk[:, g * HEAD_DIM:(g + 1) * HEAD_DIM])
        probs, dens = [], []
        for i, h in enumerate(heads):
            rows = slice(i * CHUNK, (i + 1) * CHUNK)
            s = jnp.where(prev_side, s2[rows, 0:CHUNK], s2[rows, CHUNK:2 * CHUNK]) + bias_ref[bias_idx, h]
            sink = sink_ref[h] * LOG2E
            m = jnp.maximum(jnp.max(s, axis=-1, keepdims=True), sink)
            p = jnp.exp2(s - m)
            dens.append(jnp.sum(p, axis=-1, keepdims=True) + jnp.exp2(sink - m))
            pb = p.astype(BF16)
            zero = jnp.zeros_like(pb)
            probs.append(jnp.concatenate([jnp.where(prev_side, pb, zero), jnp.where(prev_side, zero, pb)], axis=1))
        o = _dot(jnp.concatenate(probs, axis=0), vk[:, g * HEAD_DIM:(g + 1) * HEAD_DIM])
        for i in range(B_GROUP):
            outs.append(o[i * CHUNK:(i + 1) * CHUNK] * (1.0 / dens[i]))
    return jnp.concatenate(outs, axis=-1)


def _head_stack(xb, lane_head):
    zero = jnp.zeros_like(xb)
    return jnp.concatenate([jnp.where(lane_head == h, xb, zero) for h in range(C_HEADS)], axis=0)


def _hgrn_gates(cq, cf, prm, tri):
    log_lb, log_1m_lb, one_m_lb = prm
    qs = cq * _sigmoid(cq)
    e = jnp.exp(-jnp.abs(cf))
    r = 1.0 / (1.0 + e)
    ls = jnp.minimum(cf, 0.0) - jnp.log(1.0 + e)
    kin = one_m_lb * jnp.where(cf > 0.0, e * r, r)
    bq = log_1m_lb + ls
    mx = jnp.maximum(log_lb, bq)
    mn = jnp.minimum(log_lb, bq)
    lf = mx + jnp.log(1.0 + jnp.exp(mn - mx))
    hi, mid = _split2(lf)
    b = _dot(tri, hi) + _dot(tri, mid)
    return qs, kin, b


def _hgrn_scores_one_ref(qs, kin, b, lvl_ref, lane_head):
    d = b - b[HGRN_REF_ROW:HGRN_REF_ROW + 1, :]
    s = _dot_nt((qs * jnp.exp(d)).astype(BF16), _head_stack((kin * jnp.exp(-d)).astype(BF16), lane_head))
    return jnp.where(lvl_ref[...] >= 0, s, 0.0)


def _hgrn_scores_levels(qs, kin, b, lvl_ref, lane_head, row_id):
    lvl = lvl_ref[...]
    a = jnp.where(lvl == 0, _dot_nt(qs.astype(BF16), _head_stack(kin.astype(BF16), lane_head)), 0.0)
    for li, half in enumerate(HGRN_LEVELS):
        blk = 2 * half
        if blk >= 8:
            pieces = []
            for j in range(CHUNK // blk):
                r = j * blk + half - 1
                pieces.append(jnp.broadcast_to(b[r:r + 1, :], (blk, C_WIDTH)))
            bref = pieces[0] if len(pieces) == 1 else jnp.concatenate(pieces, axis=0)
        else:
            pos = row_id % blk
            bref = b
            for p in range(blk):
                shift = p - (half - 1)
                if shift == 0:
                    continue
                bref = jnp.where(pos == p, pltpu.roll(b, shift % CHUNK, axis=0), bref)
        e = jnp.exp(-jnp.abs(b - bref))
        s = _dot_nt((qs * e).astype(BF16), _head_stack((kin * e).astype(BF16), lane_head))
        a = jnp.where(lvl == li + 1, s, a)
    return a


def _hgrn_finish(a, qs, kin, b, ci, cg, st, og, lane_head, bd):
    total = b[CHUNK - 1:CHUNK, :]
    o = _dot(a, _head_stack(ci.astype(BF16), lane_head))
    o = o + _dot_nt((qs * jnp.exp(b)).astype(BF16), st.astype(BF16))
    kdec = (kin * jnp.exp(total - b)).astype(BF16)
    upd = _dot(ci.T.astype(BF16), kdec)
    vrow_head = lax.broadcasted_iota(jnp.int32, (C_WIDTH, C_WIDTH), 0) // HEAD_DIM
    kcol_head = lax.broadcasted_iota(jnp.int32, (C_WIDTH, C_WIDTH), 1) // HEAD_DIM
    st = st * jnp.exp(total) + jnp.where(vrow_head == kcol_head, upd, 0.0)
    o = o * lax.rsqrt(_seg_mean_sq(o, bd) + EPS) * og
    return o * (cg * _sigmoid(cg)), st


def _layer_kernel(x_ref, g1_ref, win_ref, wout_ref, bd_ref, aw_ref, ab_ref, ag_ref, qg_ref, kg_ref,
                  bias_ref, sink_ref, lb_ref, og_ref, lvl_ref, g2_ref, wg_ref, wu_ref, cw_ref, cb_ref, wd_ref,
                  o_ref, kv_ref, st_ref, x1_ref, gate_ref, *, rows, seq_blocks, n_blocks):
    s = pl.program_id(0)
    n_sub = rows // SUB_ROWS
    sub_chunks = SUB_ROWS // CHUNK
    mix_blk = jnp.minimum(s, n_blocks - 1)
    mix_first = mix_blk % seq_blocks == 0
    ffn_first = (s - 1) % seq_blocks == 0

    @pl.when(s == 0)
    def _():
        x1_ref[...] = jnp.zeros_like(x1_ref)

    @pl.when(mix_first)
    def _():
        kv_ref[...] = jnp.zeros_like(kv_ref)
        st_ref[...] = jnp.zeros_like(st_ref)

    @pl.when(jnp.logical_or(ffn_first, s == 0))
    def _():
        gate_ref[0:8, :] = jnp.zeros((8, D_FF), F32)

    bd = bd_ref[...]
    ab = ab_ref[...]
    lane_group = lax.broadcasted_iota(jnp.int32, (CHUNK, A_WIDTH), 1) // HEAD_DIM
    prev_side = (lax.broadcasted_iota(jnp.int32, (CHUNK, CHUNK), 1)
                 > lax.broadcasted_iota(jnp.int32, (CHUNK, CHUNK), 0))
    lb = lb_ref[...]
    prm = (jnp.log(lb), jnp.log1p(-lb), 1.0 - lb)
    tri = (lax.broadcasted_iota(jnp.int32, (CHUNK, CHUNK), 0)
           >= lax.broadcasted_iota(jnp.int32, (CHUNK, CHUNK), 1)).astype(BF16)
    lane_head = lax.broadcasted_iota(jnp.int32, (CHUNK, C_WIDTH), 1) // HEAD_DIM
    og = og_ref[...]
    st_in = st_ref[...]
    cw = cw_ref[...]
    first_bias = jnp.where(mix_first, 0, 1)

    n_chunks = n_sub * sub_chunks
    xs, hs = [None] * n_sub, [None] * n_sub
    tiles = [[None] * PROJ_TILES for _ in range(n_sub)]
    ya, yb, gates, scores = ([None] * n_chunks for _ in range(4))
    carry = {"k": kv_ref[:, 0:B_KV_WIDTH], "v": kv_ref[:, B_KV_WIDTH:2 * B_KV_WIDTH],
             "worst": jnp.zeros((1, C_WIDTH), F32)}
    pending = collections.deque()

    def emit_pending(cost, cost_left):
        for _ in range(-(-len(pending) * cost // cost_left)):
            pending.popleft()()

    def load_rows(sb):
        xs[sb] = x_ref[0, sb * SUB_ROWS:(sb + 1) * SUB_ROWS, :]
        hs[sb] = _rms_rows(xs[sb], g1_ref[...]).astype(BF16)

    def project_tile(sb, t):
        tiles[sb][t] = _dot(hs[sb], win_ref[:, t * PROJ_TILE:(t + 1) * PROJ_TILE])

    def tile_rows(sb, t, c):
        return tiles[sb][t][c * CHUNK:(c + 1) * CHUNK]

    def mixer_pieces(sb):
        local = {}

        def prep_a():
            local["u"] = _gelu(tiles[sb][TILE_AU])
            v = _gelu(tiles[sb][TILE_AV])
            local["v"] = (v * lax.rsqrt(_seg_mean_sq(v, bd) + EPS) * ag_ref[...]).astype(BF16)

        def prep_q():
            q = jnp.concatenate([tiles[sb][TILE_BQ], tiles[sb][TILE_BQ + 1]], axis=-1)
            local["q"] = (q * lax.rsqrt(_seg_mean_sq(q, bd) + EPS) * qg_ref[...]).astype(BF16)

        def prep_kv():
            k = tiles[sb][TILE_BKV][:, 0:B_KV_WIDTH]
            local["k"] = (k * lax.rsqrt(_seg_mean_sq(k, bd) + EPS) * kg_ref[...]).astype(BF16)
            local["vv"] = tiles[sb][TILE_BKV][:, B_KV_WIDTH:2 * B_KV_WIDTH].astype(BF16)

        pieces = [(2, prep_a), (1, prep_q), (1, prep_kv)]
        for c in range(sub_chunks):
            i = sb * sub_chunks + c
            sl = slice(c * CHUNK, (c + 1) * CHUNK)

            def gmlp(i=i, sl=sl):
                ya[i] = _gmlp_chunk(local["u"][sl], local["v"][sl], aw_ref, ab, lane_group)

            def attn(i=i, sl=sl):
                kc, vc = local["k"][sl], local["vv"][sl]
                bias_idx = first_bias if i == 0 else 1
                yb[i] = _attn_chunk(local["q"][sl], jnp.concatenate([carry["k"], kc], axis=0),
                                    jnp.concatenate([carry["v"], vc], axis=0), bias_ref, bias_idx, sink_ref,
                                    prev_side)
                carry["k"], carry["v"] = kc, vc

            def hgrn(i=i, c=c):
                qs, kin, b = _hgrn_gates(tile_rows(sb, TILE_CQ, c), tile_rows(sb, TILE_CF, c), prm, tri)
                gates[i] = (qs, kin, b)
                scores[i] = _hgrn_scores_one_ref(qs, kin, b, lvl_ref, lane_head).astype(BF16)
                m = b[HGRN_REF_ROW:HGRN_REF_ROW + 1, :]
                carry["worst"] = jnp.maximum(carry["worst"], jnp.maximum(-m, m - b[CHUNK - 1:CHUNK, :]))

            pieces += [(1, gmlp), (4, attn), (2, hgrn)]
        return pieces

    def finish_pieces(sb, score_list, state):
        mixed = []

        def chunk_out(c):
            i = sb * sub_chunks + c
            qs, kin, b = gates[i]
            yc, state["st"] = _hgrn_finish(score_list[i], qs, kin, b, tile_rows(sb, TILE_CI, c),
                                           tile_rows(sb, TILE_CG, c), state["st"], og, lane_head, bd)
            mixed.append(jnp.concatenate([ya[i], yb[i], yc], axis=-1).astype(BF16))

        def out_tile(t):
            m = mixed[0] if len(mixed) == 1 else jnp.concatenate(mixed, axis=0)
            cols = slice(t * PROJ_TILE, (t + 1) * PROJ_TILE)
            x1_ref[sb * SUB_ROWS:(sb + 1) * SUB_ROWS, cols] = xs[sb][:, cols] + _dot(m, wout_ref[:, cols])

        return ([functools.partial(chunk_out, c) for c in range(sub_chunks)]
                + [functools.partial(out_tile, t) for t in range(D_MODEL // PROJ_TILE)])

    def ffn_pieces(sb):
        r0 = sb * SUB_ROWS
        local = {"act": []}

        def norm():
            local["x1"] = x1_ref[r0:r0 + SUB_ROWS, :]
            local["h"] = _rms_rows(local["x1"], g2_ref[...]).astype(BF16)

        def hidden_tile(t):
            cols = slice(t * PROJ_TILE, (t + 1) * PROJ_TILE)
            gate = _dot(local["h"], wg_ref[:, cols])
            gate_ref[8 + r0:8 + r0 + SUB_ROWS, cols] = gate
            up = _dot(local["h"], wu_ref[:, cols])
            conv = (cb_ref[:, cols] + cw[2:3, cols] * gate
                    + cw[1:2, cols] * gate_ref[7 + r0:7 + r0 + SUB_ROWS, cols]
                    + cw[0:1, cols] * gate_ref[6 + r0:6 + r0 + SUB_ROWS, cols])
            local["act"].append((conv * _sigmoid(conv) * up).astype(BF16))

        def down_tile(t):
            cols = slice(t * PROJ_TILE, (t + 1) * PROJ_TILE)
            act = jnp.concatenate(local["act"], axis=-1)
            o_ref[0, r0:r0 + SUB_ROWS, cols] = local["x1"][:, cols] + _dot(act, wd_ref[:, cols])

        return ([norm] + [functools.partial(hidden_tile, t) for t in range(D_FF // PROJ_TILE)]
                + [functools.partial(down_tile, t) for t in range(D_MODEL // PROJ_TILE)])

    state = {"st": st_in}
    load_rows(0)
    for t in range(PROJ_TILES):
        project_tile(0, t)
    for sb in range(n_sub):
        if sb + 1 < n_sub:
            pending.append(functools.partial(load_rows, sb + 1))
            pending.extend(functools.partial(project_tile, sb + 1, t) for t in range(PROJ_TILES))
        pending.extend(ffn_pieces(sb))
        pieces = mixer_pieces(sb)
        cost_left = sum(cost for cost, _ in pieces)
        for cost, piece in pieces:
            piece()
            emit_pending(cost, cost_left)
            cost_left -= cost
        pending.extend(finish_pieces(sb, scores, state))
    while pending:
        pending.popleft()()
    st_ref[...] = state["st"]
    kv_ref[:, 0:B_KV_WIDTH] = carry["k"]
    kv_ref[:, B_KV_WIDTH:2 * B_KV_WIDTH] = carry["v"]
    gate_ref[0:8, :] = gate_ref[rows:rows + 8, :]

    @pl.when(jnp.max(carry["worst"], axis=-1, keepdims=True)[0, 0] > HGRN_SAFE_LOG_RANGE)
    def _():
        row_id = lax.broadcasted_iota(jnp.int32, (CHUNK, C_WIDTH), 0)
        safe = [_hgrn_scores_levels(qs, kin, b, lvl_ref, lane_head, row_id).astype(BF16) for qs, kin, b in gates]
        redo = {"st": st_in}
        for sb in range(n_sub):
            for piece in finish_pieces(sb, safe, redo):
                piece()
        st_ref[...] = redo["st"]


class _LayerOf(NamedTuple):
    stack: jax.Array
    layer: int


def _operand(c):
    return c.stack if isinstance(c, _LayerOf) else c


def _const_spec(c):
    if isinstance(c, _LayerOf):
        nd = c.stack.ndim - 1
        return pl.BlockSpec((None,) + c.stack.shape[1:], lambda s, _l=c.layer, _nd=nd: (_l,) + (0,) * _nd,
                            pipeline_mode=pl.Buffered(1))
    nd = c.ndim
    return pl.BlockSpec(c.shape, lambda s, _nd=nd: (0,) * _nd, pipeline_mode=pl.Buffered(1))


def _layer(x, p, rows):
    bsz, seq, _ = x.shape
    seq_blocks = seq // rows
    n_blocks = bsz * seq_blocks

    def block_of(blk):
        return blk // seq_blocks, blk % seq_blocks, 0

    x_spec = pl.BlockSpec((1, rows, D_MODEL), lambda s: block_of(jnp.minimum(s, n_blocks - 1)))
    o_spec = pl.BlockSpec((1, rows, D_MODEL), lambda s: block_of(jnp.maximum(s - 1, 0)))
    consts = [p["g1"], p["w_in"], p["w_out"], p["bd"], p["a_w"], p["a_b"], p["a_g"], p["q_g"], p["k_g"],
              p["bias"]]
    tail = [p["lb"], p["o_g"], p["lvl"], p["g2"], p["w_gate"], p["w_up"], p["conv_w"], p["conv_b"], p["w_down"]]
    in_specs = ([x_spec] + [_const_spec(c) for c in consts]
                + [pl.BlockSpec(memory_space=pltpu.SMEM)] + [_const_spec(c) for c in tail])
    return pl.pallas_call(
        functools.partial(_layer_kernel, rows=rows, seq_blocks=seq_blocks, n_blocks=n_blocks),
        grid=(n_blocks + 1,),
        in_specs=in_specs,
        out_specs=o_spec,
        out_shape=jax.ShapeDtypeStruct(x.shape, x.dtype),
        scratch_shapes=[pltpu.VMEM((CHUNK, 2 * B_KV_WIDTH), BF16),
                        pltpu.VMEM((C_WIDTH, C_WIDTH), F32),
                        pltpu.VMEM((rows, D_MODEL), F32),
                        pltpu.VMEM((rows + 8, D_FF), F32)],
        compiler_params=pltpu.CompilerParams(
            dimension_semantics=("arbitrary",), vmem_limit_bytes=VMEM_LIMIT_BYTES),
        name="decoder_layer",
    )(x, *map(_operand, consts), p["sinks"], *map(_operand, tail))


def _t5_causal_bucket_table():
    qi = np.arange(CHUNK)[:, None]
    kj = np.arange(CHUNK)[None, :]
    prev_side = kj > qi
    dist = np.where(prev_side, qi + CHUNK - kj, qi - kj)
    max_exact = REL_BUCKETS // 2
    nf = np.maximum(dist, 1).astype(np.float64)
    large = max_exact + (np.log(nf / max_exact) / math.log(REL_MAX_DIST / max_exact)
                         * (REL_BUCKETS - max_exact)).astype(np.int32)
    large = np.minimum(large, REL_BUCKETS - 1)
    bucket = np.where(dist < max_exact, dist, large).astype(np.int32)
    return bucket, prev_side


def _attention_bias_tables(rel_bias):
    bucket, prev_side = _t5_causal_bucket_table()
    onehot = (jnp.asarray(bucket.reshape(-1))[:, None] == jnp.arange(REL_BUCKETS)[None, :]).astype(F32)
    bias = jnp.einsum("pb,bh->hp", onehot, rel_bias.astype(F32), precision=lax.Precision.HIGHEST)
    bias = bias.reshape(B_Q_HEADS, CHUNK, CHUNK) * LOG2E
    first = jnp.where(jnp.asarray(prev_side)[None], MASK_VALUE, bias)
    return jnp.stack([first, bias])


def _level_table():
    t = np.arange(CHUNK)[:, None]
    s = np.arange(CHUNK)[None, :]
    x = np.bitwise_xor(t, s)
    msb = np.floor(np.log2(np.maximum(x, 1))).astype(np.int32)
    lvl = np.where(s == t, 0, np.where(s < t, len(HGRN_LEVELS) - msb, -1)).astype(np.int32)
    return np.tile(lvl, (1, C_HEADS))


def _block_diag_ones():
    i = np.arange(256)
    return (i[:, None] // HEAD_DIM == i[None, :] // HEAD_DIM).astype(np.float32)


def kernel(x, norm1_g, w_in, gmlp_vnorm_g, gmlp_w_s, gmlp_b_s, q_norm_g, k_norm_g, attn_sinks, rel_bias,
           hgrn_lb_logits, hgrn_onorm_g, w_out, norm2_g, w_gate, w_up, conv_w, conv_b, w_down):
    depth = w_in.shape[0]
    seq = x.shape[1]
    rows = min(LAYER_ROWS, seq)
    lb_cum = jnp.cumsum(jax.nn.softmax(hgrn_lb_logits.astype(F32), axis=0), axis=0)
    lower_bounds = lb_cum - lb_cum[0]
    bias_tables = _attention_bias_tables(rel_bias)
    causal = jnp.tril(jnp.ones((CHUNK, CHUNK), dtype=bool))
    bd = jnp.asarray(_block_diag_ones(), BF16)
    lvl = jnp.asarray(_level_table())
    w_in, w_out, w_gate, w_up, w_down = (w.astype(BF16) for w in (w_in, w_out, w_gate, w_up, w_down))
    for l in range(depth):
        p = {
            "g1": norm1_g[l].reshape(1, D_MODEL),
            "w_in": _LayerOf(w_in, l),
            "w_out": _LayerOf(w_out, l),
            "bd": bd,
            "a_w": jnp.where(causal, gmlp_w_s[l], 0.0).astype(BF16).reshape(A_GROUPS * CHUNK, CHUNK),
            "a_b": jnp.repeat(gmlp_b_s[l].T, HEAD_DIM, axis=1),
            "a_g": gmlp_vnorm_g[l].reshape(1, A_WIDTH),
            "q_g": jnp.tile(q_norm_g[l], B_Q_HEADS).reshape(1, B_WIDTH) * (HEAD_DIM ** -0.5 * LOG2E),
            "k_g": jnp.tile(k_norm_g[l], B_KV_HEADS).reshape(1, B_KV_WIDTH),
            "bias": bias_tables,
            "sinks": attn_sinks[l].astype(F32),
            "lb": lower_bounds[l].reshape(1, C_WIDTH),
            "o_g": jnp.tile(hgrn_onorm_g[l], C_HEADS).reshape(1, C_WIDTH),
            "lvl": lvl,
            "g2": norm2_g[l].reshape(1, D_MODEL),
            "w_gate": _LayerOf(w_gate, l),
            "w_up": _LayerOf(w_up, l),
            "conv_w": conv_w[l],
            "conv_b": conv_b[l].reshape(1, D_FF),
            "w_down": _LayerOf(w_down, l),
        }
        x = _layer(x, p, rows)
    return x
```

```python
import collections
import functools
import math
from typing import NamedTuple

import numpy as np
import jax
import jax.numpy as jnp
from jax import lax
from jax.experimental import pallas as pl
from jax.experimental.pallas import tpu as pltpu

F32 = jnp.float32
BF16 = jnp.bfloat16

D_MODEL = 1024
HEAD_DIM = 64
CHUNK = 128
A_GROUPS = 4
A_WIDTH = 256
B_Q_HEADS = 8
B_KV_HEADS = 2
B_GROUP = 4
B_WIDTH = 512
B_KV_WIDTH = 128
REL_BUCKETS = 32
REL_MAX_DIST = 128
C_HEADS = 4
C_WIDTH = 256
IN_WIDTH = 2304
D_FF = 2816
CONV_WIDTH = 3
EPS = 1e-6
MASK_VALUE = -1e30
LOG2E = math.log2(math.e)

PROJ_TILE = 256
PROJ_TILES = IN_WIDTH // PROJ_TILE
TILE_AU, TILE_AV, TILE_BQ, TILE_BKV, TILE_CQ, TILE_CF, TILE_CI, TILE_CG = 0, 1, 2, 4, 5, 6, 7, 8

HGRN_LEVELS = (64, 32, 16, 8, 4, 2, 1)
HGRN_REF_ROW = CHUNK // 2 - 1
HGRN_SAFE_LOG_RANGE = 80.0

MIXER_ROWS = 1024
SUB_ROWS = 256
FFN_ROWS = 512
VMEM_LIMIT_BYTES = 56 * 1024 * 1024

_dot = functools.partial(jnp.dot, preferred_element_type=F32)
_dot_nt = functools.partial(lax.dot_general, dimension_numbers=(((1,), (1,)), ((), ())),
                            preferred_element_type=F32)


def _split2(x):
    hi = x.astype(BF16)
    mid = (x - hi.astype(F32)).astype(BF16)
    return hi, mid


def _seg_mean_sq(x, bd):
    sq = (x * x).astype(BF16)
    w = x.shape[-1]
    outs = []
    for c0 in range(0, w, 256):
        cw = min(256, w - c0)
        outs.append(_dot(sq[:, c0:c0 + cw], bd[0:cw, 0:cw]))
    out = outs[0] if len(outs) == 1 else jnp.concatenate(outs, axis=-1)
    return out * (1.0 / HEAD_DIM)


def _sigmoid(x):
    return 1.0 / (1.0 + jnp.exp(-x))


def _gelu(x):
    return 0.5 * x * (1.0 + lax.erf(x * (1.0 / math.sqrt(2.0))))


def _rms_rows(x, g):
    ms = jnp.mean(x * x, axis=-1, keepdims=True)
    return x * lax.rsqrt(ms + EPS) * g


def _gmlp_chunk(u, vb, aw_ref, ab, lane_group):
    r = _dot(aw_ref[...], vb)
    sv = r[0:CHUNK]
    for g in range(1, A_GROUPS):
        sv = jnp.where(lane_group == g, r[g * CHUNK:(g + 1) * CHUNK], sv)
    return u * (sv + ab)


def _attn_chunk(qb, kt, vk, bias_ref, bias_idx, sink_ref, prev_side):
    outs = []
    for g in range(B_KV_HEADS):
        heads = range(g * B_GROUP, (g + 1) * B_GROUP)
        q_stack = jnp.concatenate([qb[:, h * HEAD_DIM:(h + 1) * HEAD_DIM] for h in heads], axis=0)
        s2 = _dot(q_stack, kt[g * HEAD_DIM:(g + 1) * HEAD_DIM, :])
        probs, dens = [], []
        for i, h in enumerate(heads):
            rows = slice(i * CHUNK, (i + 1) * CHUNK)
            s = jnp.where(prev_side, s2[rows, 0:CHUNK], s2[rows, CHUNK:2 * CHUNK]) + bias_ref[bias_idx, h]
            sink = sink_ref[h] * LOG2E
            m = jnp.maximum(jnp.max(s, axis=-1, keepdims=True), sink)
            p = jnp.exp2(s - m)
            dens.append(jnp.sum(p, axis=-1, keepdims=True) + jnp.exp2(sink - m))
            pb = p.astype(BF16)
            zero = jnp.zeros_like(pb)
            probs.append(jnp.concatenate([jnp.where(prev_side, pb, zero), jnp.where(prev_side, zero, pb)], axis=1))
        o = _dot(jnp.concatenate(probs, axis=0), vk[:, g * HEAD_DIM:(g + 1) * HEAD_DIM])
        for i in range(B_GROUP):
            outs.append(o[i * CHUNK:(i + 1) * CHUNK] * (1.0 / dens[i]))
    return jnp.concatenate(outs, axis=-1)


def _head_stack(xb, lane_head):
    zero = jnp.zeros_like(xb)
    return jnp.concatenate([jnp.where(lane_head == h, xb, zero) for h in range(C_HEADS)], axis=0)


def _hgrn_gates(cq, cf, prm, tri):
    log_lb, log_1m_lb, one_m_lb = prm
    qs = cq * _sigmoid(cq)
    e = jnp.exp(-jnp.abs(cf))
    r = 1.0 / (1.0 + e)
    ls = jnp.minimum(cf, 0.0) - jnp.log(1.0 + e)
    kin = one_m_lb * jnp.where(cf > 0.0, e * r, r)
    bq = log_1m_lb + ls
    mx = jnp.maximum(log_lb, bq)
    mn = jnp.minimum(log_lb, bq)
    lf = mx + jnp.log(1.0 + jnp.exp(mn - mx))
    hi, mid = _split2(lf)
    b = _dot(tri, hi) + _dot(tri, mid)
    return qs, kin, b


def _hgrn_scores_one_ref(qs, kin, b, lvl_ref):
    d = b - b[HGRN_REF_ROW:HGRN_REF_ROW + 1, :]
    kt = (kin * jnp.exp(-d)).T.astype(BF16)
    row_head = lax.broadcasted_iota(jnp.int32, (C_WIDTH, CHUNK), 0) // HEAD_DIM
    zero = jnp.zeros_like(kt)
    kt_stack = jnp.concatenate([jnp.where(row_head == h, kt, zero) for h in range(C_HEADS)], axis=1)
    s = _dot((qs * jnp.exp(d)).astype(BF16), kt_stack)
    return jnp.where(lvl_ref[...] >= 0, s, 0.0)


def _hgrn_scores_levels(qs, kin, b, lvl_ref, lane_head, row_id):
    lvl = lvl_ref[...]
    a = jnp.where(lvl == 0, _dot_nt(qs.astype(BF16), _head_stack(kin.astype(BF16), lane_head)), 0.0)
    for li, half in enumerate(HGRN_LEVELS):
        blk = 2 * half
        if blk >= 8:
            pieces = []
            for j in range(CHUNK // blk):
                r = j * blk + half - 1
                pieces.append(jnp.broadcast_to(b[r:r + 1, :], (blk, C_WIDTH)))
            bref = pieces[0] if len(pieces) == 1 else jnp.concatenate(pieces, axis=0)
        else:
            pos = row_id % blk
            bref = b
            for p in range(blk):
                shift = p - (half - 1)
                if shift == 0:
                    continue
                bref = jnp.where(pos == p, pltpu.roll(b, shift % CHUNK, axis=0), bref)
        e = jnp.exp(-jnp.abs(b - bref))
        s = _dot_nt((qs * e).astype(BF16), _head_stack((kin * e).astype(BF16), lane_head))
        a = jnp.where(lvl == li + 1, s, a)
    return a


def _hgrn_finish(a, qs, kin, b, ci, cg, st, og, lane_head, bd):
    total = b[CHUNK - 1:CHUNK, :]
    o = _dot(a, _head_stack(ci.astype(BF16), lane_head))
    o = o + _dot((qs * jnp.exp(b)).astype(BF16), st.astype(BF16))
    kdec_t = (kin * jnp.exp(total - b)).T
    upd = _dot(kdec_t.astype(BF16), ci.astype(BF16))
    decay = jnp.exp(jnp.broadcast_to(total, (8, C_WIDTH))).T[:, 0:1]
    krow_head = lax.broadcasted_iota(jnp.int32, (C_WIDTH, C_WIDTH), 0) // HEAD_DIM
    vcol_head = lax.broadcasted_iota(jnp.int32, (C_WIDTH, C_WIDTH), 1) // HEAD_DIM
    st = st * decay + jnp.where(krow_head == vcol_head, upd, 0.0)
    o = o * lax.rsqrt(_seg_mean_sq(o, bd) + EPS) * og
    return o * (cg * _sigmoid(cg)), st


def _mixer_kernel(x_ref, g1_ref, win_ref, wout_ref, bd_ref, aw_ref, ab_ref, ag_ref, qg_ref, kg_ref,
                  bias_ref, sink_ref, lb_ref, og_ref, lvl_ref, o_ref, kv_ref, st_ref, *, rows):
    j = pl.program_id(1)
    n_sub = rows // SUB_ROWS
    sub_chunks = SUB_ROWS // CHUNK

    @pl.when(j == 0)
    def _():
        kv_ref[...] = jnp.zeros_like(kv_ref)
        st_ref[...] = jnp.zeros_like(st_ref)

    bd = bd_ref[...]
    ab = ab_ref[...]
    lane_group = lax.broadcasted_iota(jnp.int32, (CHUNK, A_WIDTH), 1) // HEAD_DIM
    prev_side = (lax.broadcasted_iota(jnp.int32, (CHUNK, CHUNK), 1)
                 > lax.broadcasted_iota(jnp.int32, (CHUNK, CHUNK), 0))
    lb = lb_ref[...]
    prm = (jnp.log(lb), jnp.log1p(-lb), 1.0 - lb)
    tri = (lax.broadcasted_iota(jnp.int32, (CHUNK, CHUNK), 0)
           >= lax.broadcasted_iota(jnp.int32, (CHUNK, CHUNK), 1)).astype(BF16)
    lane_head = lax.broadcasted_iota(jnp.int32, (CHUNK, C_WIDTH), 1) // HEAD_DIM
    og = og_ref[...]
    st_in = st_ref[...]

    n_chunks = n_sub * sub_chunks
    xs, hs = [None] * n_sub, [None] * n_sub
    tiles = [[None] * PROJ_TILES for _ in range(n_sub)]
    ya, yb, gates, scores = ([None] * n_chunks for _ in range(4))
    carry = {"k": kv_ref[:, 0:CHUNK], "v": kv_ref[:, CHUNK:CHUNK + B_KV_WIDTH],
             "worst": jnp.zeros((1, C_WIDTH), F32)}
    pending = collections.deque()

    def emit_pending(cost, cost_left):
        for _ in range(-(-len(pending) * cost // cost_left)):
            pending.popleft()()

    def load_rows(sb):
        xs[sb] = x_ref[0, sb * SUB_ROWS:(sb + 1) * SUB_ROWS, :]
        hs[sb] = _rms_rows(xs[sb], g1_ref[...]).astype(BF16)

    def project_tile(sb, t):
        tiles[sb][t] = _dot(hs[sb], win_ref[:, t * PROJ_TILE:(t + 1) * PROJ_TILE])

    def tile_rows(sb, t, c):
        return tiles[sb][t][c * CHUNK:(c + 1) * CHUNK]

    def mixer_pieces(sb):
        local = {}

        def prep_a():
            local["u"] = _gelu(tiles[sb][TILE_AU])
            v = _gelu(tiles[sb][TILE_AV])
            local["v"] = (v * lax.rsqrt(_seg_mean_sq(v, bd) + EPS) * ag_ref[...]).astype(BF16)

        def prep_q():
            q = jnp.concatenate([tiles[sb][TILE_BQ], tiles[sb][TILE_BQ + 1]], axis=-1)
            local["q"] = (q * lax.rsqrt(_seg_mean_sq(q, bd) + EPS) * qg_ref[...]).astype(BF16)

        def prep_kv():
            k = tiles[sb][TILE_BKV][:, 0:B_KV_WIDTH]
            local["k"] = k * lax.rsqrt(_seg_mean_sq(k, bd) + EPS) * kg_ref[...]
            local["vv"] = tiles[sb][TILE_BKV][:, B_KV_WIDTH:2 * B_KV_WIDTH].astype(BF16)

        pieces = [(2, prep_a), (1, prep_q), (1, prep_kv)]
        for c in range(sub_chunks):
            i = sb * sub_chunks + c
            sl = slice(c * CHUNK, (c + 1) * CHUNK)

            def gmlp(i=i, sl=sl):
                ya[i] = _gmlp_chunk(local["u"][sl], local["v"][sl], aw_ref, ab, lane_group)

            def attn(i=i, sl=sl):
                kc, vc = local["k"][sl].T.astype(BF16), local["vv"][sl]
                bias_idx = jnp.where(j == 0, 0, 1) if i == 0 else 1
                yb[i] = _attn_chunk(local["q"][sl], jnp.concatenate([carry["k"], kc], axis=1),
                                    jnp.concatenate([carry["v"], vc], axis=0), bias_ref, bias_idx, sink_ref,
                                    prev_side)
                carry["k"], carry["v"] = kc, vc

            def hgrn(i=i, c=c):
                qs, kin, b = _hgrn_gates(tile_rows(sb, TILE_CQ, c), tile_rows(sb, TILE_CF, c), prm, tri)
                gates[i] = (qs, kin, b)
                scores[i] = _hgrn_scores_one_ref(qs, kin, b, lvl_ref).astype(BF16)
                m = b[HGRN_REF_ROW:HGRN_REF_ROW + 1, :]
                carry["worst"] = jnp.maximum(carry["worst"], jnp.maximum(-m, m - b[CHUNK - 1:CHUNK, :]))

            pieces += [(1, gmlp), (4, attn), (2, hgrn)]
        return pieces

    def finish_pieces(sb, score_list, state):
        mixed = []

        def chunk_out(c):
            i = sb * sub_chunks + c
            qs, kin, b = gates[i]
            yc, state["st"] = _hgrn_finish(score_list[i], qs, kin, b, tile_rows(sb, TILE_CI, c),
                                           tile_rows(sb, TILE_CG, c), state["st"], og, lane_head, bd)
            mixed.append(jnp.concatenate([ya[i], yb[i], yc], axis=-1).astype(BF16))

        def out_tile(t):
            m = mixed[0] if len(mixed) == 1 else jnp.concatenate(mixed, axis=0)
            cols = slice(t * PROJ_TILE, (t + 1) * PROJ_TILE)
            o_ref[0, sb * SUB_ROWS:(sb + 1) * SUB_ROWS, cols] = xs[sb][:, cols] + _dot(m, wout_ref[:, cols])

        return ([functools.partial(chunk_out, c) for c in range(sub_chunks)]
                + [functools.partial(out_tile, t) for t in range(D_MODEL // PROJ_TILE)])

    state = {"st": st_in}
    load_rows(0)
    for t in range(PROJ_TILES):
        project_tile(0, t)
    for sb in range(n_sub):
        if sb + 1 < n_sub:
            pending.append(functools.partial(load_rows, sb + 1))
            pending.extend(functools.partial(project_tile, sb + 1, t) for t in range(PROJ_TILES))
        pieces = mixer_pieces(sb)
        cost_left = sum(cost for cost, _ in pieces)
        for cost, piece in pieces:
            piece()
            emit_pending(cost, cost_left)
            cost_left -= cost
        pending.extend(finish_pieces(sb, scores, state))
    while pending:
        pending.popleft()()
    st_ref[...] = state["st"]
    kv_ref[:, 0:CHUNK] = carry["k"]
    kv_ref[:, CHUNK:CHUNK + B_KV_WIDTH] = carry["v"]

    @pl.when(jnp.max(carry["worst"], axis=-1, keepdims=True)[0, 0] > HGRN_SAFE_LOG_RANGE)
    def _():
        row_id = lax.broadcasted_iota(jnp.int32, (CHUNK, C_WIDTH), 0)
        safe = [_hgrn_scores_levels(qs, kin, b, lvl_ref, lane_head, row_id).astype(BF16) for qs, kin, b in gates]
        redo = {"st": st_in}
        for sb in range(n_sub):
            for piece in finish_pieces(sb, safe, redo):
                piece()
        st_ref[...] = redo["st"]


class _LayerOf(NamedTuple):
    stack: jax.Array
    layer: int


def _operand(c):
    return c.stack if isinstance(c, _LayerOf) else c


def _const_spec(c):
    if isinstance(c, _LayerOf):
        nd = c.stack.ndim - 1
        return pl.BlockSpec((None,) + c.stack.shape[1:], lambda b, j, _l=c.layer, _nd=nd: (_l,) + (0,) * _nd,
                            pipeline_mode=pl.Buffered(1))
    nd = c.ndim
    return pl.BlockSpec(c.shape, lambda b, j, _nd=nd: (0,) * _nd, pipeline_mode=pl.Buffered(1))


def _mixer_layer(x, p, rows):
    bsz, seq, _ = x.shape
    grid = (bsz, seq // rows)
    x_spec = pl.BlockSpec((1, rows, D_MODEL), lambda b, j: (b, j, 0))
    consts = [p["g1"], p["w_in"], p["w_out"], p["bd"], p["a_w"], p["a_b"], p["a_g"], p["q_g"], p["k_g"],
              p["bias"]]
    tail = [p["lb"], p["o_g"], p["lvl"]]
    in_specs = ([x_spec] + [_const_spec(c) for c in consts]
                + [pl.BlockSpec(memory_space=pltpu.SMEM)] + [_const_spec(c) for c in tail])
    return pl.pallas_call(
        functools.partial(_mixer_kernel, rows=rows),
        grid=grid,
        in_specs=in_specs,
        out_specs=x_spec,
        out_shape=jax.ShapeDtypeStruct(x.shape, x.dtype),
        scratch_shapes=[pltpu.VMEM((CHUNK, CHUNK + B_KV_WIDTH), BF16),
                        pltpu.VMEM((C_WIDTH, C_WIDTH), F32)],
        compiler_params=pltpu.CompilerParams(
            dimension_semantics=("arbitrary", "arbitrary"), vmem_limit_bytes=VMEM_LIMIT_BYTES),
        name="mixer_layer",
    )(x, *map(_operand, consts), p["sinks"], *tail)


def _ffn_kernel(x_ref, g2_ref, wg_ref, wu_ref, cw_ref, cb_ref, wd_ref, o_ref, gate_ref, *, rows):
    j = pl.program_id(1)

    @pl.when(j == 0)
    def _():
        gate_ref[0:8, :] = jnp.zeros((8, D_FF), F32)

    cw = cw_ref[...]
    for sb in range(rows // SUB_ROWS):
        r0 = sb * SUB_ROWS
        x = x_ref[0, r0:r0 + SUB_ROWS, :]
        h = _rms_rows(x, g2_ref[...]).astype(BF16)
        gate = _dot(h, wg_ref[...])
        gate_ref[8 + r0:8 + r0 + SUB_ROWS, :] = gate
        up = _dot(h, wu_ref[...])
        conv = (cb_ref[...] + cw[2:3, :] * gate + cw[1:2, :] * gate_ref[7 + r0:7 + r0 + SUB_ROWS, :]
                + cw[0:1, :] * gate_ref[6 + r0:6 + r0 + SUB_ROWS, :])
        act = (conv * _sigmoid(conv) * up).astype(BF16)
        o_ref[0, r0:r0 + SUB_ROWS, :] = x + _dot(act, wd_ref[...])
    gate_ref[0:8, :] = gate_ref[rows:rows + 8, :]


def _ffn_layer(x, p, rows):
    bsz, seq, _ = x.shape
    grid = (bsz, seq // rows)
    x_spec = pl.BlockSpec((1, rows, D_MODEL), lambda b, j: (b, j, 0))
    consts = [p["g2"], p["w_gate"], p["w_up"], p["conv_w"], p["conv_b"], p["w_down"]]
    return pl.pallas_call(
        functools.partial(_ffn_kernel, rows=rows),
        grid=grid,
        in_specs=[x_spec] + [_const_spec(c) for c in consts],
        out_specs=x_spec,
        out_shape=jax.ShapeDtypeStruct(x.shape, x.dtype),
        scratch_shapes=[pltpu.VMEM((rows + 8, D_FF), F32)],
        compiler_params=pltpu.CompilerParams(
            dimension_semantics=("arbitrary", "arbitrary"), vmem_limit_bytes=VMEM_LIMIT_BYTES),
        name="ffn_layer",
    )(x, *map(_operand, consts))


def _t5_causal_bucket_table():
    qi = np.arange(CHUNK)[:, None]
    kj = np.arange(CHUNK)[None, :]
    prev_side = kj > qi
    dist = np.where(prev_side, qi + CHUNK - kj, qi - kj)
    max_exact = REL_BUCKETS // 2
    nf = np.maximum(dist, 1).astype(np.float64)
    large = max_exact + (np.log(nf / max_exact) / math.log(REL_MAX_DIST / max_exact)
                         * (REL_BUCKETS - max_exact)).astype(np.int32)
    large = np.minimum(large, REL_BUCKETS - 1)
    bucket = np.where(dist < max_exact, dist, large).astype(np.int32)
    return bucket, prev_side


def _attention_bias_tables(rel_bias):
    bucket, prev_side = _t5_causal_bucket_table()
    onehot = (jnp.asarray(bucket.reshape(-1))[:, None] == jnp.arange(REL_BUCKETS)[None, :]).astype(F32)
    bias = jnp.einsum("pb,bh->hp", onehot, rel_bias.astype(F32), precision=lax.Precision.HIGHEST)
    bias = bias.reshape(B_Q_HEADS, CHUNK, CHUNK) * LOG2E
    first = jnp.where(jnp.asarray(prev_side)[None], MASK_VALUE, bias)
    return jnp.stack([first, bias])


def _level_table():
    t = np.arange(CHUNK)[:, None]
    s = np.arange(CHUNK)[None, :]
    x = np.bitwise_xor(t, s)
    msb = np.floor(np.log2(np.maximum(x, 1))).astype(np.int32)
    lvl = np.where(s == t, 0, np.where(s < t, len(HGRN_LEVELS) - msb, -1)).astype(np.int32)
    return np.tile(lvl, (1, C_HEADS))


def _block_diag_ones():
    i = np.arange(256)
    return (i[:, None] // HEAD_DIM == i[None, :] // HEAD_DIM).astype(np.float32)


def kernel(x, norm1_g, w_in, gmlp_vnorm_g, gmlp_w_s, gmlp_b_s, q_norm_g, k_norm_g, attn_sinks, rel_bias,
           hgrn_lb_logits, hgrn_onorm_g, w_out, norm2_g, w_gate, w_up, conv_w, conv_b, w_down):
    depth = w_in.shape[0]
    seq = x.shape[1]
    mixer_rows = min(MIXER_ROWS, seq)
    ffn_rows = min(FFN_ROWS, seq)
    lb_cum = jnp.cumsum(jax.nn.softmax(hgrn_lb_logits.astype(F32), axis=0), axis=0)
    lower_bounds = lb_cum - lb_cum[0]
    bias_tables = _attention_bias_tables(rel_bias)
    causal = jnp.tril(jnp.ones((CHUNK, CHUNK), dtype=bool))
    bd = jnp.asarray(_block_diag_ones(), BF16)
    lvl = jnp.asarray(_level_table())
    w_in, w_out, w_gate, w_up, w_down = (w.astype(BF16) for w in (w_in, w_out, w_gate, w_up, w_down))
    for l in range(depth):
        p = {
            "g1": norm1_g[l].reshape(1, D_MODEL),
            "w_in": _LayerOf(w_in, l),
            "w_out": _LayerOf(w_out, l),
            "bd": bd,
            "a_w": jnp.where(causal, gmlp_w_s[l], 0.0).astype(BF16).reshape(A_GROUPS * CHUNK, CHUNK),
            "a_b": jnp.repeat(gmlp_b_s[l].T, HEAD_DIM, axis=1),
            "a_g": gmlp_vnorm_g[l].reshape(1, A_WIDTH),
            "q_g": jnp.tile(q_norm_g[l], B_Q_HEADS).reshape(1, B_WIDTH) * (HEAD_DIM ** -0.5 * LOG2E),
            "k_g": jnp.tile(k_norm_g[l], B_KV_HEADS).reshape(1, B_KV_WIDTH),
            "bias": bias_tables,
            "sinks": attn_sinks[l].astype(F32),
            "lb": lower_bounds[l].reshape(1, C_WIDTH),
            "o_g": jnp.tile(hgrn_onorm_g[l], C_HEADS).reshape(1, C_WIDTH),
            "lvl": lvl,
            "g2": norm2_g[l].reshape(1, D_MODEL),
            "w_gate": _LayerOf(w_gate, l),
            "w_up": _LayerOf(w_up, l),
            "conv_w": conv_w[l],
            "conv_b": conv_b[l].reshape(1, D_FF),
            "w_down": _LayerOf(w_down, l),
        }
        x = _mixer_layer(x, p, mixer_rows)
        x = _ffn_layer(x, p, ffn_rows)
    return x
```

```python
import collections
import functools
import math
from typing import NamedTuple

import numpy as np
import jax
import jax.numpy as jnp
from jax import lax
from jax.experimental import pallas as pl
from jax.experimental.pallas import tpu as pltpu

F32 = jnp.float32
BF16 = jnp.bfloat16

D_MODEL = 1024
HEAD_DIM = 64
CHUNK = 128
A_GROUPS = 4
A_WIDTH = 256
B_Q_HEADS = 8
B_KV_HEADS = 2
B_GROUP = 4
B_WIDTH = 512
B_KV_WIDTH = 128
REL_BUCKETS = 32
REL_MAX_DIST = 128
C_HEADS = 4
C_WIDTH = 256
IN_WIDTH = 2304
D_FF = 2816
CONV_WIDTH = 3
EPS = 1e-6
MASK_VALUE = -1e30
LOG2E = math.log2(math.e)

PROJ_TILE = 256
PROJ_TILES = IN_WIDTH // PROJ_TILE
TILE_AU, TILE_AV, TILE_BQ, TILE_BKV, TILE_CQ, TILE_CF, TILE_CI, TILE_CG = 0, 1, 2, 4, 5, 6, 7, 8

HGRN_LEVELS = (64, 32, 16, 8, 4, 2, 1)
HGRN_REF_ROW = CHUNK // 2 - 1
HGRN_SAFE_LOG_RANGE = 80.0

MIXER_ROWS = 1024
SUB_ROWS = 256
FFN_ROWS = 1024
FFN_SUB_ROWS = 512
VMEM_LIMIT_BYTES = 56 * 1024 * 1024

_dot = functools.partial(jnp.dot, preferred_element_type=F32)
_dot_nt = functools.partial(lax.dot_general, dimension_numbers=(((1,), (1,)), ((), ())),
                            preferred_element_type=F32)


def _split2(x):
    hi = x.astype(BF16)
    mid = (x - hi.astype(F32)).astype(BF16)
    return hi, mid


def _seg_mean_sq(x, bd):
    sq = (x * x).astype(BF16)
    w = x.shape[-1]
    outs = []
    for c0 in range(0, w, 256):
        cw = min(256, w - c0)
        outs.append(_dot(sq[:, c0:c0 + cw], bd[0:cw, 0:cw]))
    out = outs[0] if len(outs) == 1 else jnp.concatenate(outs, axis=-1)
    return out * (1.0 / HEAD_DIM)


def _sigmoid(x):
    return 1.0 / (1.0 + jnp.exp(-x))


def _silu(x):
    hx = 0.5 * x
    return hx + hx * jnp.tanh(hx)


def _gelu(x):
    return 0.5 * x * (1.0 + lax.erf(x * (1.0 / math.sqrt(2.0))))


def _rms_rows(x, g):
    ms = jnp.mean(x * x, axis=-1, keepdims=True)
    return x * lax.rsqrt(ms + EPS) * g


def _gmlp_chunk(u, vb, aw_ref, ab, lane_group):
    r = _dot(aw_ref[...], vb)
    sv = r[0:CHUNK]
    for g in range(1, A_GROUPS):
        sv = jnp.where(lane_group == g, r[g * CHUNK:(g + 1) * CHUNK], sv)
    return u * (sv + ab)


def _attn_chunk(qb, kk, vk, bias_ref, bias_idx, sink_ref, prev_side):
    outs = []
    for g in range(B_KV_HEADS):
        heads = range(g * B_GROUP, (g + 1) * B_GROUP)
        q_stack = jnp.concatenate([qb[:, h * HEAD_DIM:(h + 1) * HEAD_DIM] for h in heads], axis=0)
        s2 = _dot_nt(q_stack, kk[:, g * HEAD_DIM:(g + 1) * HEAD_DIM])
        probs, dens = [], []
        for i, h in enumerate(heads):
            rows = slice(i * CHUNK, (i + 1) * CHUNK)
            s = jnp.where(prev_side, s2[rows, 0:CHUNK], s2[rows, CHUNK:2 * CHUNK]) + bias_ref[bias_idx, h]
            sink = sink_ref[h] * LOG2E
            m = jnp.maximum(jnp.max(s, axis=-1, keepdims=True), sink)
            p = jnp.exp2(s - m)
            dens.append(jnp.sum(p, axis=-1, keepdims=True) + jnp.exp2(sink - m))
            pb = p.astype(BF16)
            zero = jnp.zeros_like(pb)
            probs.append(jnp.concatenate([jnp.where(prev_side, pb, zero), jnp.where(prev_side, zero, pb)], axis=1))
        o = _dot(jnp.concatenate(probs, axis=0), vk[:, g * HEAD_DIM:(g + 1) * HEAD_DIM])
        for i in range(B_GROUP):
            outs.append(o[i * CHUNK:(i + 1) * CHUNK] * (1.0 / dens[i]))
    return jnp.concatenate(outs, axis=-1)


def _head_stack(xb, lane_head):
    zero = jnp.zeros_like(xb)
    return jnp.concatenate([jnp.where(lane_head == h, xb, zero) for h in range(C_HEADS)], axis=0)


def _hgrn_gates(cq, cf, prm, tri):
    log_lb, log_1m_lb, one_m_lb = prm
    qs = cq * _sigmoid(cq)
    e = jnp.exp(-jnp.abs(cf))
    r = 1.0 / (1.0 + e)
    ls = jnp.minimum(cf, 0.0) - jnp.log(1.0 + e)
    kin = one_m_lb * jnp.where(cf > 0.0, e * r, r)
    bq = log_1m_lb + ls
    mx = jnp.maximum(log_lb, bq)
    mn = jnp.minimum(log_lb, bq)
    lf = mx + jnp.log(1.0 + jnp.exp(mn - mx))
    hi, mid = _split2(lf)
    b = _dot(tri, hi) + _dot(tri, mid)
    return qs, kin, b


def _hgrn_scores_one_ref(qs, kin, b, lvl_ref, lane_head):
    d = b - b[HGRN_REF_ROW:HGRN_REF_ROW + 1, :]
    s = _dot_nt((qs * jnp.exp(d)).astype(BF16), _head_stack((kin * jnp.exp(-d)).astype(BF16), lane_head))
    return jnp.where(lvl_ref[...] >= 0, s, 0.0)


def _hgrn_scores_levels(qs, kin, b, lvl_ref, lane_head, row_id):
    lvl = lvl_ref[...]
    a = jnp.where(lvl == 0, _dot_nt(qs.astype(BF16), _head_stack(kin.astype(BF16), lane_head)), 0.0)
    for li, half in enumerate(HGRN_LEVELS):
        blk = 2 * half
        if blk >= 8:
            pieces = []
            for j in range(CHUNK // blk):
                r = j * blk + half - 1
                pieces.append(jnp.broadcast_to(b[r:r + 1, :], (blk, C_WIDTH)))
            bref = pieces[0] if len(pieces) == 1 else jnp.concatenate(pieces, axis=0)
        else:
            pos = row_id % blk
            bref = b
            for p in range(blk):
                shift = p - (half - 1)
                if shift == 0:
                    continue
                bref = jnp.where(pos == p, pltpu.roll(b, shift % CHUNK, axis=0), bref)
        e = jnp.exp(-jnp.abs(b - bref))
        s = _dot_nt((qs * e).astype(BF16), _head_stack((kin * e).astype(BF16), lane_head))
        a = jnp.where(lvl == li + 1, s, a)
    return a


def _hgrn_finish(a, qs, kin, b, ci, cg, st, og, lane_head, bd):
    total = b[CHUNK - 1:CHUNK, :]
    o = _dot(a, _head_stack(ci.astype(BF16), lane_head))
    o = o + _dot_nt((qs * jnp.exp(b)).astype(BF16), st.astype(BF16))
    kdec = (kin * jnp.exp(total - b)).astype(BF16)
    upd = _dot(ci.T.astype(BF16), kdec)
    vrow_head = lax.broadcasted_iota(jnp.int32, (C_WIDTH, C_WIDTH), 0) // HEAD_DIM
    kcol_head = lax.broadcasted_iota(jnp.int32, (C_WIDTH, C_WIDTH), 1) // HEAD_DIM
    st = st * jnp.exp(total) + jnp.where(vrow_head == kcol_head, upd, 0.0)
    o = o * lax.rsqrt(_seg_mean_sq(o, bd) + EPS) * og
    return o * (cg * _sigmoid(cg)), st


def _mixer_kernel(x_ref, g1_ref, win_ref, wout_ref, bd_ref, aw_ref, ab_ref, ag_ref, qg_ref, kg_ref,
                  bias_ref, sink_ref, lb_ref, og_ref, lvl_ref, o_ref, kv_ref, st_ref, *, rows):
    j = pl.program_id(1)
    n_sub = rows // SUB_ROWS
    sub_chunks = SUB_ROWS // CHUNK

    @pl.when(j == 0)
    def _():
        kv_ref[...] = jnp.zeros_like(kv_ref)
        st_ref[...] = jnp.zeros_like(st_ref)

    bd = bd_ref[...]
    ab = ab_ref[...]
    lane_group = lax.broadcasted_iota(jnp.int32, (CHUNK, A_WIDTH), 1) // HEAD_DIM
    prev_side = (lax.broadcasted_iota(jnp.int32, (CHUNK, CHUNK), 1)
                 > lax.broadcasted_iota(jnp.int32, (CHUNK, CHUNK), 0))
    lb = lb_ref[...]
    prm = (jnp.log(lb), jnp.log1p(-lb), 1.0 - lb)
    tri = (lax.broadcasted_iota(jnp.int32, (CHUNK, CHUNK), 0)
           >= lax.broadcasted_iota(jnp.int32, (CHUNK, CHUNK), 1)).astype(BF16)
    lane_head = lax.broadcasted_iota(jnp.int32, (CHUNK, C_WIDTH), 1) // HEAD_DIM
    og = og_ref[...]
    st_in = st_ref[...]

    n_chunks = n_sub * sub_chunks
    xs, hs = [None] * n_sub, [None] * n_sub
    tiles = [[None] * PROJ_TILES for _ in range(n_sub)]
    ya, yb, gates, scores = ([None] * n_chunks for _ in range(4))
    carry = {"k": kv_ref[:, 0:B_KV_WIDTH], "v": kv_ref[:, B_KV_WIDTH:2 * B_KV_WIDTH],
             "worst": jnp.zeros((1, C_WIDTH), F32)}
    pending = collections.deque()

    def emit_pending(cost, cost_left):
        for _ in range(-(-len(pending) * cost // cost_left)):
            pending.popleft()()

    def load_rows(sb):
        xs[sb] = x_ref[0, sb * SUB_ROWS:(sb + 1) * SUB_ROWS, :]
        hs[sb] = _rms_rows(xs[sb], g1_ref[...]).astype(BF16)

    def project_tile(sb, t):
        tiles[sb][t] = _dot(hs[sb], win_ref[:, t * PROJ_TILE:(t + 1) * PROJ_TILE])

    def tile_rows(sb, t, c):
        return tiles[sb][t][c * CHUNK:(c + 1) * CHUNK]

    def mixer_pieces(sb):
        local = {}

        def prep_a():
            local["u"] = _gelu(tiles[sb][TILE_AU])
            v = _gelu(tiles[sb][TILE_AV])
            local["v"] = (v * lax.rsqrt(_seg_mean_sq(v, bd) + EPS) * ag_ref[...]).astype(BF16)

        def prep_q():
            q = jnp.concatenate([tiles[sb][TILE_BQ], tiles[sb][TILE_BQ + 1]], axis=-1)
            local["q"] = (q * lax.rsqrt(_seg_mean_sq(q, bd) + EPS) * qg_ref[...]).astype(BF16)

        def prep_kv():
            k = tiles[sb][TILE_BKV][:, 0:B_KV_WIDTH]
            local["k"] = (k * lax.rsqrt(_seg_mean_sq(k, bd) + EPS) * kg_ref[...]).astype(BF16)
            local["vv"] = tiles[sb][TILE_BKV][:, B_KV_WIDTH:2 * B_KV_WIDTH].astype(BF16)

        pieces = [(2, prep_a), (1, prep_q), (1, prep_kv)]
        for c in range(sub_chunks):
            i = sb * sub_chunks + c
            sl = slice(c * CHUNK, (c + 1) * CHUNK)

            def gmlp(i=i, sl=sl):
                ya[i] = _gmlp_chunk(local["u"][sl], local["v"][sl], aw_ref, ab, lane_group)

            def attn(i=i, sl=sl):
                kc, vc = local["k"][sl], local["vv"][sl]
                bias_idx = jnp.where(j == 0, 0, 1) if i == 0 else 1
                yb[i] = _attn_chunk(local["q"][sl], jnp.concatenate([carry["k"], kc], axis=0),
                                    jnp.concatenate([carry["v"], vc], axis=0), bias_ref, bias_idx, sink_ref,
                                    prev_side)
                carry["k"], carry["v"] = kc, vc

            def hgrn(i=i, c=c):
                qs, kin, b = _hgrn_gates(tile_rows(sb, TILE_CQ, c), tile_rows(sb, TILE_CF, c), prm, tri)
                gates[i] = (qs, kin, b)
                scores[i] = _hgrn_scores_one_ref(qs, kin, b, lvl_ref, lane_head).astype(BF16)
                m = b[HGRN_REF_ROW:HGRN_REF_ROW + 1, :]
                carry["worst"] = jnp.maximum(carry["worst"], jnp.maximum(-m, m - b[CHUNK - 1:CHUNK, :]))

            pieces += [(1, gmlp), (4, attn), (2, hgrn)]
        return pieces

    def finish_pieces(sb, score_list, state):
        mixed = []

        def chunk_out(c):
            i = sb * sub_chunks + c
            qs, kin, b = gates[i]
            yc, state["st"] = _hgrn_finish(score_list[i], qs, kin, b, tile_rows(sb, TILE_CI, c),
                                           tile_rows(sb, TILE_CG, c), state["st"], og, lane_head, bd)
            mixed.append(jnp.concatenate([ya[i], yb[i], yc], axis=-1).astype(BF16))

        def out_tile(t):
            m = mixed[0] if len(mixed) == 1 else jnp.concatenate(mixed, axis=0)
            cols = slice(t * PROJ_TILE, (t + 1) * PROJ_TILE)
            o_ref[0, sb * SUB_ROWS:(sb + 1) * SUB_ROWS, cols] = xs[sb][:, cols] + _dot(m, wout_ref[:, cols])

        return ([functools.partial(chunk_out, c) for c in range(sub_chunks)]
                + [functools.partial(out_tile, t) for t in range(D_MODEL // PROJ_TILE)])

    state = {"st": st_in}
    load_rows(0)
    for t in range(PROJ_TILES):
        project_tile(0, t)
    for sb in range(n_sub):
        if sb + 1 < n_sub:
            pending.append(functools.partial(load_rows, sb + 1))
            pending.extend(functools.partial(project_tile, sb + 1, t) for t in range(PROJ_TILES))
        pieces = mixer_pieces(sb)
        cost_left = sum(cost for cost, _ in pieces)
        for cost, piece in pieces:
            piece()
            emit_pending(cost, cost_left)
            cost_left -= cost
        pending.extend(finish_pieces(sb, scores, state))
    while pending:
        pending.popleft()()
    st_ref[...] = state["st"]
    kv_ref[:, 0:B_KV_WIDTH] = carry["k"]
    kv_ref[:, B_KV_WIDTH:2 * B_KV_WIDTH] = carry["v"]

    @pl.when(jnp.max(carry["worst"], axis=-1, keepdims=True)[0, 0] > HGRN_SAFE_LOG_RANGE)
    def _():
        row_id = lax.broadcasted_iota(jnp.int32, (CHUNK, C_WIDTH), 0)
        safe = [_hgrn_scores_levels(qs, kin, b, lvl_ref, lane_head, row_id).astype(BF16) for qs, kin, b in gates]
        redo = {"st": st_in}
        for sb in range(n_sub):
            for piece in finish_pieces(sb, safe, redo):
                piece()
        st_ref[...] = redo["st"]


class _LayerOf(NamedTuple):
    stack: jax.Array
    layer: int


def _operand(c):
    return c.stack if isinstance(c, _LayerOf) else c


def _const_spec(c):
    if isinstance(c, _LayerOf):
        nd = c.stack.ndim - 1
        return pl.BlockSpec((None,) + c.stack.shape[1:], lambda b, j, _l=c.layer, _nd=nd: (_l,) + (0,) * _nd,
                            pipeline_mode=pl.Buffered(1))
    nd = c.ndim
    return pl.BlockSpec(c.shape, lambda b, j, _nd=nd: (0,) * _nd, pipeline_mode=pl.Buffered(1))


def _mixer_layer(x, p, rows):
    bsz, seq, _ = x.shape
    grid = (bsz, seq // rows)
    x_spec = pl.BlockSpec((1, rows, D_MODEL), lambda b, j: (b, j, 0))
    consts = [p["g1"], p["w_in"], p["w_out"], p["bd"], p["a_w"], p["a_b"], p["a_g"], p["q_g"], p["k_g"],
              p["bias"]]
    tail = [p["lb"], p["o_g"], p["lvl"]]
    in_specs = ([x_spec] + [_const_spec(c) for c in consts]
                + [pl.BlockSpec(memory_space=pltpu.SMEM)] + [_const_spec(c) for c in tail])
    return pl.pallas_call(
        functools.partial(_mixer_kernel, rows=rows),
        grid=grid,
        in_specs=in_specs,
        out_specs=x_spec,
        out_shape=jax.ShapeDtypeStruct(x.shape, x.dtype),
        scratch_shapes=[pltpu.VMEM((CHUNK, 2 * B_KV_WIDTH), BF16),
                        pltpu.VMEM((C_WIDTH, C_WIDTH), F32)],
        compiler_params=pltpu.CompilerParams(
            dimension_semantics=("arbitrary", "arbitrary"), vmem_limit_bytes=VMEM_LIMIT_BYTES),
        name="mixer_layer",
    )(x, *map(_operand, consts), p["sinks"], *tail)


def _ffn_kernel(x_ref, g2_ref, wg_ref, wu_ref, cw_ref, cb_ref, wd_ref, o_ref, gate_ref, *, rows):
    j = pl.program_id(1)

    @pl.when(j == 0)
    def _():
        gate_ref[0:8, :] = jnp.zeros((8, D_FF), F32)

    cw = cw_ref[...]
    sub = min(FFN_SUB_ROWS, rows)
    for sb in range(rows // sub):
        r0 = sb * sub
        x = x_ref[0, r0:r0 + sub, :]
        h = _rms_rows(x, g2_ref[...]).astype(BF16)
        gate = _dot(h, wg_ref[...])
        gate_ref[8 + r0:8 + r0 + sub, :] = gate
        up = _dot(h, wu_ref[...])
        conv = (cb_ref[...] + cw[2:3, :] * gate + cw[1:2, :] * gate_ref[7 + r0:7 + r0 + sub, :]
                + cw[0:1, :] * gate_ref[6 + r0:6 + r0 + sub, :])
        act = (_silu(conv) * up).astype(BF16)
        o_ref[0, r0:r0 + sub, :] = x + _dot(act, wd_ref[...])
    gate_ref[0:8, :] = gate_ref[rows:rows + 8, :]


def _ffn_layer(x, p, rows):
    bsz, seq, _ = x.shape
    grid = (bsz, seq // rows)
    x_spec = pl.BlockSpec((1, rows, D_MODEL), lambda b, j: (b, j, 0))
    consts = [p["g2"], p["w_gate"], p["w_up"], p["conv_w"], p["conv_b"], p["w_down"]]
    return pl.pallas_call(
        functools.partial(_ffn_kernel, rows=rows),
        grid=grid,
        in_specs=[x_spec] + [_const_spec(c) for c in consts],
        out_specs=x_spec,
        out_shape=jax.ShapeDtypeStruct(x.shape, x.dtype),
        scratch_shapes=[pltpu.VMEM((rows + 8, D_FF), F32)],
        compiler_params=pltpu.CompilerParams(
            dimension_semantics=("arbitrary", "arbitrary"), vmem_limit_bytes=VMEM_LIMIT_BYTES),
        name="ffn_layer",
    )(x, *map(_operand, consts))


def _t5_causal_bucket_table():
    qi = np.arange(CHUNK)[:, None]
    kj = np.arange(CHUNK)[None, :]
    prev_side = kj > qi
    dist = np.where(prev_side, qi + CHUNK - kj, qi - kj)
    max_exact = REL_BUCKETS // 2
    nf = np.maximum(dist, 1).astype(np.float64)
    large = max_exact + (np.log(nf / max_exact) / math.log(REL_MAX_DIST / max_exact)
                         * (REL_BUCKETS - max_exact)).astype(np.int32)
    large = np.minimum(large, REL_BUCKETS - 1)
    bucket = np.where(dist < max_exact, dist, large).astype(np.int32)
    return bucket, prev_side


def _attention_bias_tables(rel_bias):
    bucket, prev_side = _t5_causal_bucket_table()
    onehot = (jnp.asarray(bucket.reshape(-1))[:, None] == jnp.arange(REL_BUCKETS)[None, :]).astype(F32)
    bias = jnp.einsum("pb,bh->hp", onehot, rel_bias.astype(F32), precision=lax.Precision.HIGHEST)
    bias = bias.reshape(B_Q_HEADS, CHUNK, CHUNK) * LOG2E
    first = jnp.where(jnp.asarray(prev_side)[None], MASK_VALUE, bias)
    return jnp.stack([first, bias])


def _level_table():
    t = np.arange(CHUNK)[:, None]
    s = np.arange(CHUNK)[None, :]
    x = np.bitwise_xor(t, s)
    msb = np.floor(np.log2(np.maximum(x, 1))).astype(np.int32)
    lvl = np.where(s == t, 0, np.where(s < t, len(HGRN_LEVELS) - msb, -1)).astype(np.int32)
    return np.tile(lvl, (1, C_HEADS))


def _block_diag_ones():
    i = np.arange(256)
    return (i[:, None] // HEAD_DIM == i[None, :] // HEAD_DIM).astype(np.float32)


def kernel(x, norm1_g, w_in, gmlp_vnorm_g, gmlp_w_s, gmlp_b_s, q_norm_g, k_norm_g, attn_sinks, rel_bias,
           hgrn_lb_logits, hgrn_onorm_g, w_out, norm2_g, w_gate, w_up, conv_w, conv_b, w_down):
    depth = w_in.shape[0]
    seq = x.shape[1]
    mixer_rows = min(MIXER_ROWS, seq)
    ffn_rows = min(FFN_ROWS, seq)
    lb_cum = jnp.cumsum(jax.nn.softmax(hgrn_lb_logits.astype(F32), axis=0), axis=0)
    lower_bounds = lb_cum - lb_cum[0]
    bias_tables = _attention_bias_tables(rel_bias)
    causal = jnp.tril(jnp.ones((CHUNK, CHUNK), dtype=bool))
    bd = jnp.asarray(_block_diag_ones(), BF16)
    lvl = jnp.asarray(_level_table())
    w_in, w_out, w_gate, w_up, w_down = (w.astype(BF16) for w in (w_in, w_out, w_gate, w_up, w_down))
    for l in range(depth):
        p = {
            "g1": norm1_g[l].reshape(1, D_MODEL),
            "w_in": _LayerOf(w_in, l),
            "w_out": _LayerOf(w_out, l),
            "bd": bd,
            "a_w": jnp.where(causal, gmlp_w_s[l], 0.0).astype(BF16).reshape(A_GROUPS * CHUNK, CHUNK),
            "a_b": jnp.repeat(gmlp_b_s[l].T, HEAD_DIM, axis=1),
            "a_g": gmlp_vnorm_g[l].reshape(1, A_WIDTH),
            "q_g": jnp.tile(q_norm_g[l], B_Q_HEADS).reshape(1, B_WIDTH) * (HEAD_DIM ** -0.5 * LOG2E),
            "k_g": jnp.tile(k_norm_g[l], B_KV_HEADS).reshape(1, B_KV_WIDTH),
            "bias": bias_tables,
            "sinks": attn_sinks[l].astype(F32),
            "lb": lower_bounds[l].reshape(1, C_WIDTH),
            "o_g": jnp.tile(hgrn_onorm_g[l], C_HEADS).reshape(1, C_WIDTH),
            "lvl": lvl,
            "g2": norm2_g[l].reshape(1, D_MODEL),
            "w_gate": _LayerOf(w_gate, l),
            "w_up": _LayerOf(w_up, l),
            "conv_w": conv_w[l],
            "conv_b": conv_b[l].reshape(1, D_FF),
            "w_down": _LayerOf(w_down, l),
        }
        x = _mixer_layer(x, p, mixer_rows)
        x = _ffn_layer(x, p, ffn_rows)
    return x
```

```python
import collections
import functools
import math
from typing import NamedTuple

import numpy as np
import jax
import jax.numpy as jnp
from jax import lax
from jax.experimental import pallas as pl
from jax.experimental.pallas import tpu as pltpu

F32 = jnp.float32
BF16 = jnp.bfloat16

D_MODEL = 1024
HEAD_DIM = 64
CHUNK = 128
A_GROUPS = 4
A_WIDTH = 256
B_Q_HEADS = 8
B_KV_HEADS = 2
B_GROUP = 4
B_WIDTH = 512
B_KV_WIDTH = 128
REL_BUCKETS = 32
REL_MAX_DIST = 128
C_HEADS = 4
C_WIDTH = 256
IN_WIDTH = 2304
D_FF = 2816
CONV_WIDTH = 3
EPS = 1e-6
MASK_VALUE = -1e30
LOG2E = math.log2(math.e)

PROJ_TILE = 256
PROJ_TILES = IN_WIDTH // PROJ_TILE
TILE_AU, TILE_AV, TILE_BQ, TILE_BKV, TILE_CQ, TILE_CF, TILE_CI, TILE_CG = 0, 1, 2, 4, 5, 6, 7, 8

HGRN_LEVELS = (64, 32, 16, 8, 4, 2, 1)
HGRN_REF_ROW = CHUNK // 2 - 1
HGRN_SAFE_LOG_RANGE = 80.0

MIXER_ROWS = 1024
SUB_ROWS = 256
FFN_ROWS = 512
FFN_SUB_ROWS = 256
VMEM_LIMIT_BYTES = 56 * 1024 * 1024

_dot = functools.partial(jnp.dot, preferred_element_type=F32)
_dot_nt = functools.partial(lax.dot_general, dimension_numbers=(((1,), (1,)), ((), ())),
                            preferred_element_type=F32)


def _split2(x):
    hi = x.astype(BF16)
    mid = (x - hi.astype(F32)).astype(BF16)
    return hi, mid


def _seg_mean_sq(x, bd):
    sq = (x * x).astype(BF16)
    w = x.shape[-1]
    outs = []
    for c0 in range(0, w, 256):
        cw = min(256, w - c0)
        outs.append(_dot(sq[:, c0:c0 + cw], bd[0:cw, 0:cw]))
    out = outs[0] if len(outs) == 1 else jnp.concatenate(outs, axis=-1)
    return out * (1.0 / HEAD_DIM)


def _sigmoid(x):
    return 1.0 / (1.0 + jnp.exp(-x))


def _silu(x):
    hx = 0.5 * x
    return hx + hx * jnp.tanh(hx)


def _gelu(x):
    return 0.5 * x * (1.0 + lax.erf(x * (1.0 / math.sqrt(2.0))))


def _rms_rows(x, g):
    ms = jnp.mean(x * x, axis=-1, keepdims=True)
    return x * lax.rsqrt(ms + EPS) * g


def _gmlp_chunk(u, vb, aw_ref, ab, lane_group):
    r = _dot(aw_ref[...], vb)
    sv = r[0:CHUNK]
    for g in range(1, A_GROUPS):
        sv = jnp.where(lane_group == g, r[g * CHUNK:(g + 1) * CHUNK], sv)
    return u * (sv + ab)


def _attn_chunk(qb, kk, vk, bias_ref, bias_idx, sink_ref, prev_side):
    outs = []
    for g in range(B_KV_HEADS):
        heads = range(g * B_GROUP, (g + 1) * B_GROUP)
        q_stack = jnp.concatenate([qb[:, h * HEAD_DIM:(h + 1) * HEAD_DIM] for h in heads], axis=0)
        s2 = _dot_nt(q_stack, kk[:, g * HEAD_DIM:(g + 1) * HEAD_DIM])
        probs, dens = [], []
        for i, h in enumerate(heads):
            rows = slice(i * CHUNK, (i + 1) * CHUNK)
            s = jnp.where(prev_side, s2[rows, 0:CHUNK], s2[rows, CHUNK:2 * CHUNK]) + bias_ref[bias_idx, h]
            sink = sink_ref[h] * LOG2E
            m = jnp.maximum(jnp.max(s, axis=-1, keepdims=True), sink)
            p = jnp.exp2(s - m)
            dens.append(jnp.sum(p, axis=-1, keepdims=True) + jnp.exp2(sink - m))
            pb = p.astype(BF16)
            zero = jnp.zeros_like(pb)
            probs.append(jnp.concatenate([jnp.where(prev_side, pb, zero), jnp.where(prev_side, zero, pb)], axis=1))
        o = _dot(jnp.concatenate(probs, axis=0), vk[:, g * HEAD_DIM:(g + 1) * HEAD_DIM])
        for i in range(B_GROUP):
            outs.append(o[i * CHUNK:(i + 1) * CHUNK] * (1.0 / dens[i]))
    return jnp.concatenate(outs, axis=-1)


def _head_stack(xb, lane_head):
    zero = jnp.zeros_like(xb)
    return jnp.concatenate([jnp.where(lane_head == h, xb, zero) for h in range(C_HEADS)], axis=0)


def _hgrn_gates(cq, cf, prm, tri):
    log_lb, log_1m_lb, one_m_lb = prm
    qs = cq * _sigmoid(cq)
    e = jnp.exp(-jnp.abs(cf))
    r = 1.0 / (1.0 + e)
    ls = jnp.minimum(cf, 0.0) - jnp.log(1.0 + e)
    kin = one_m_lb * jnp.where(cf > 0.0, e * r, r)
    bq = log_1m_lb + ls
    mx = jnp.maximum(log_lb, bq)
    mn = jnp.minimum(log_lb, bq)
    lf = mx + jnp.log(1.0 + jnp.exp(mn - mx))
    hi, mid = _split2(lf)
    b = _dot(tri, hi) + _dot(tri, mid)
    return qs, kin, b


def _hgrn_scores_one_ref(qs, kin, b, lvl_ref, lane_head):
    d = b - b[HGRN_REF_ROW:HGRN_REF_ROW + 1, :]
    s = _dot_nt((qs * jnp.exp(d)).astype(BF16), _head_stack((kin * jnp.exp(-d)).astype(BF16), lane_head))
    return jnp.where(lvl_ref[...] >= 0, s, 0.0)


def _hgrn_scores_levels(qs, kin, b, lvl_ref, lane_head, row_id):
    lvl = lvl_ref[...]
    a = jnp.where(lvl == 0, _dot_nt(qs.astype(BF16), _head_stack(kin.astype(BF16), lane_head)), 0.0)
    for li, half in enumerate(HGRN_LEVELS):
        blk = 2 * half
        if blk >= 8:
            pieces = []
            for j in range(CHUNK // blk):
                r = j * blk + half - 1
                pieces.append(jnp.broadcast_to(b[r:r + 1, :], (blk, C_WIDTH)))
            bref = pieces[0] if len(pieces) == 1 else jnp.concatenate(pieces, axis=0)
        else:
            pos = row_id % blk
            bref = b
            for p in range(blk):
                shift = p - (half - 1)
                if shift == 0:
                    continue
                bref = jnp.where(pos == p, pltpu.roll(b, shift % CHUNK, axis=0), bref)
        e = jnp.exp(-jnp.abs(b - bref))
        s = _dot_nt((qs * e).astype(BF16), _head_stack((kin * e).astype(BF16), lane_head))
        a = jnp.where(lvl == li + 1, s, a)
    return a


def _hgrn_finish(a, qs, kin, b, ci, cg, st, og, lane_head, bd):
    total = b[CHUNK - 1:CHUNK, :]
    o = _dot(a, _head_stack(ci.astype(BF16), lane_head))
    o = o + _dot_nt((qs * jnp.exp(b)).astype(BF16), st.astype(BF16))
    kdec = (kin * jnp.exp(total - b)).astype(BF16)
    upd = _dot(ci.T.astype(BF16), kdec)
    vrow_head = lax.broadcasted_iota(jnp.int32, (C_WIDTH, C_WIDTH), 0) // HEAD_DIM
    kcol_head = lax.broadcasted_iota(jnp.int32, (C_WIDTH, C_WIDTH), 1) // HEAD_DIM
    st = st * jnp.exp(total) + jnp.where(vrow_head == kcol_head, upd, 0.0)
    o = o * lax.rsqrt(_seg_mean_sq(o, bd) + EPS) * og
    return o * (cg * _sigmoid(cg)), st


def _mixer_kernel(x_ref, g1_ref, win_ref, wout_ref, bd_ref, aw_ref, ab_ref, ag_ref, qg_ref, kg_ref,
                  bias_ref, sink_ref, lb_ref, og_ref, lvl_ref, o_ref, kv_ref, st_ref, *, rows):
    j = pl.program_id(1)
    n_sub = rows // SUB_ROWS
    sub_chunks = SUB_ROWS // CHUNK

    @pl.when(j == 0)
    def _():
        kv_ref[...] = jnp.zeros_like(kv_ref)
        st_ref[...] = jnp.zeros_like(st_ref)

    bd = bd_ref[...]
    ab = ab_ref[...]
    lane_group = lax.broadcasted_iota(jnp.int32, (CHUNK, A_WIDTH), 1) // HEAD_DIM
    prev_side = (lax.broadcasted_iota(jnp.int32, (CHUNK, CHUNK), 1)
                 > lax.broadcasted_iota(jnp.int32, (CHUNK, CHUNK), 0))
    lb = lb_ref[...]
    prm = (jnp.log(lb), jnp.log1p(-lb), 1.0 - lb)
    tri = (lax.broadcasted_iota(jnp.int32, (CHUNK, CHUNK), 0)
           >= lax.broadcasted_iota(jnp.int32, (CHUNK, CHUNK), 1)).astype(BF16)
    lane_head = lax.broadcasted_iota(jnp.int32, (CHUNK, C_WIDTH), 1) // HEAD_DIM
    og = og_ref[...]
    st_in = st_ref[...]

    n_chunks = n_sub * sub_chunks
    xs, hs = [None] * n_sub, [None] * n_sub
    tiles = [[None] * PROJ_TILES for _ in range(n_sub)]
    ya, yb, gates, scores = ([None] * n_chunks for _ in range(4))
    carry = {"k": kv_ref[:, 0:B_KV_WIDTH], "v": kv_ref[:, B_KV_WIDTH:2 * B_KV_WIDTH],
             "worst": jnp.zeros((1, C_WIDTH), F32)}
    pending = collections.deque()

    def emit_pending(cost, cost_left):
        for _ in range(-(-len(pending) * cost // cost_left)):
            pending.popleft()()

    def load_rows(sb):
        xs[sb] = x_ref[0, sb * SUB_ROWS:(sb + 1) * SUB_ROWS, :]
        hs[sb] = _rms_rows(xs[sb], g1_ref[...]).astype(BF16)

    def project_tile(sb, t):
        tiles[sb][t] = _dot(hs[sb], win_ref[:, t * PROJ_TILE:(t + 1) * PROJ_TILE])

    def tile_rows(sb, t, c):
        return tiles[sb][t][c * CHUNK:(c + 1) * CHUNK]

    def mixer_pieces(sb):
        local = {}

        def prep_a():
            local["u"] = _gelu(tiles[sb][TILE_AU])
            v = _gelu(tiles[sb][TILE_AV])
            local["v"] = (v * lax.rsqrt(_seg_mean_sq(v, bd) + EPS) * ag_ref[...]).astype(BF16)

        def prep_q():
            q = jnp.concatenate([tiles[sb][TILE_BQ], tiles[sb][TILE_BQ + 1]], axis=-1)
            local["q"] = (q * lax.rsqrt(_seg_mean_sq(q, bd) + EPS) * qg_ref[...]).astype(BF16)

        def prep_kv():
            k = tiles[sb][TILE_BKV][:, 0:B_KV_WIDTH]
            local["k"] = (k * lax.rsqrt(_seg_mean_sq(k, bd) + EPS) * kg_ref[...]).astype(BF16)
            local["vv"] = tiles[sb][TILE_BKV][:, B_KV_WIDTH:2 * B_KV_WIDTH].astype(BF16)

        pieces = [(2, prep_a), (1, prep_q), (1, prep_kv)]
        for c in range(sub_chunks):
            i = sb * sub_chunks + c
            sl = slice(c * CHUNK, (c + 1) * CHUNK)

            def gmlp(i=i, sl=sl):
                ya[i] = _gmlp_chunk(local["u"][sl], local["v"][sl], aw_ref, ab, lane_group)

            def attn(i=i, sl=sl):
                kc, vc = local["k"][sl], local["vv"][sl]
                bias_idx = jnp.where(j == 0, 0, 1) if i == 0 else 1
                yb[i] = _attn_chunk(local["q"][sl], jnp.concatenate([carry["k"], kc], axis=0),
                                    jnp.concatenate([carry["v"], vc], axis=0), bias_ref, bias_idx, sink_ref,
                                    prev_side)
                carry["k"], carry["v"] = kc, vc

            def hgrn(i=i, c=c):
                qs, kin, b = _hgrn_gates(tile_rows(sb, TILE_CQ, c), tile_rows(sb, TILE_CF, c), prm, tri)
                gates[i] = (qs, kin, b)
                scores[i] = _hgrn_scores_one_ref(qs, kin, b, lvl_ref, lane_head).astype(BF16)
                m = b[HGRN_REF_ROW:HGRN_REF_ROW + 1, :]
                carry["worst"] = jnp.maximum(carry["worst"], jnp.maximum(-m, m - b[CHUNK - 1:CHUNK, :]))

            pieces += [(1, gmlp), (4, attn), (2, hgrn)]
        return pieces

    def finish_pieces(sb, score_list, state):
        mixed = []

        def chunk_out(c):
            i = sb * sub_chunks + c
            qs, kin, b = gates[i]
            yc, state["st"] = _hgrn_finish(score_list[i], qs, kin, b, tile_rows(sb, TILE_CI, c),
                                           tile_rows(sb, TILE_CG, c), state["st"], og, lane_head, bd)
            mixed.append(jnp.concatenate([ya[i], yb[i], yc], axis=-1).astype(BF16))

        def out_tile(t):
            m = mixed[0] if len(mixed) == 1 else jnp.concatenate(mixed, axis=0)
            cols = slice(t * PROJ_TILE, (t + 1) * PROJ_TILE)
            o_ref[0, sb * SUB_ROWS:(sb + 1) * SUB_ROWS, cols] = xs[sb][:, cols] + _dot(m, wout_ref[:, cols])

        return ([functools.partial(chunk_out, c) for c in range(sub_chunks)]
                + [functools.partial(out_tile, t) for t in range(D_MODEL // PROJ_TILE)])

    state = {"st": st_in}
    load_rows(0)
    for t in range(PROJ_TILES):
        project_tile(0, t)
    for sb in range(n_sub):
        if sb + 1 < n_sub:
            pending.append(functools.partial(load_rows, sb + 1))
            pending.extend(functools.partial(project_tile, sb + 1, t) for t in range(PROJ_TILES))
        pieces = mixer_pieces(sb)
        cost_left = sum(cost for cost, _ in pieces)
        for cost, piece in pieces:
            piece()
            emit_pending(cost, cost_left)
            cost_left -= cost
        pending.extend(finish_pieces(sb, scores, state))
    while pending:
        pending.popleft()()
    st_ref[...] = state["st"]
    kv_ref[:, 0:B_KV_WIDTH] = carry["k"]
    kv_ref[:, B_KV_WIDTH:2 * B_KV_WIDTH] = carry["v"]

    @pl.when(jnp.max(carry["worst"], axis=-1, keepdims=True)[0, 0] > HGRN_SAFE_LOG_RANGE)
    def _():
        row_id = lax.broadcasted_iota(jnp.int32, (CHUNK, C_WIDTH), 0)
        safe = [_hgrn_scores_levels(qs, kin, b, lvl_ref, lane_head, row_id).astype(BF16) for qs, kin, b in gates]
        redo = {"st": st_in}
        for sb in range(n_sub):
            for piece in finish_pieces(sb, safe, redo):
                piece()
        st_ref[...] = redo["st"]


class _LayerOf(NamedTuple):
    stack: jax.Array
    layer: int


def _operand(c):
    return c.stack if isinstance(c, _LayerOf) else c


def _const_spec(c):
    if isinstance(c, _LayerOf):
        nd = c.stack.ndim - 1
        return pl.BlockSpec((None,) + c.stack.shape[1:], lambda b, j, _l=c.layer, _nd=nd: (_l,) + (0,) * _nd,
                            pipeline_mode=pl.Buffered(1))
    nd = c.ndim
    return pl.BlockSpec(c.shape, lambda b, j, _nd=nd: (0,) * _nd, pipeline_mode=pl.Buffered(1))


def _mixer_layer(x, p, rows):
    bsz, seq, _ = x.shape
    grid = (bsz, seq // rows)
    x_spec = pl.BlockSpec((1, rows, D_MODEL), lambda b, j: (b, j, 0))
    consts = [p["g1"], p["w_in"], p["w_out"], p["bd"], p["a_w"], p["a_b"], p["a_g"], p["q_g"], p["k_g"],
              p["bias"]]
    tail = [p["lb"], p["o_g"], p["lvl"]]
    in_specs = ([x_spec] + [_const_spec(c) for c in consts]
                + [pl.BlockSpec(memory_space=pltpu.SMEM)] + [_const_spec(c) for c in tail])
    return pl.pallas_call(
        functools.partial(_mixer_kernel, rows=rows),
        grid=grid,
        in_specs=in_specs,
        out_specs=x_spec,
        out_shape=jax.ShapeDtypeStruct(x.shape, x.dtype),
        scratch_shapes=[pltpu.VMEM((CHUNK, 2 * B_KV_WIDTH), BF16),
                        pltpu.VMEM((C_WIDTH, C_WIDTH), F32)],
        compiler_params=pltpu.CompilerParams(
            dimension_semantics=("arbitrary", "arbitrary"), vmem_limit_bytes=VMEM_LIMIT_BYTES),
        name="mixer_layer",
    )(x, *map(_operand, consts), p["sinks"], *tail)


def _ffn_kernel(x_ref, g2_ref, wg_ref, wu_ref, cw_ref, cb_ref, wd_ref, o_ref, gate_ref, *, rows):
    j = pl.program_id(1)

    @pl.when(j == 0)
    def _():
        gate_ref[0:8, :] = jnp.zeros((8, D_FF), F32)

    cw = cw_ref[...]
    sub = min(FFN_SUB_ROWS, rows)
    for sb in range(rows // sub):
        r0 = sb * sub
        x = x_ref[0, r0:r0 + sub, :]
        h = _rms_rows(x, g2_ref[...]).astype(BF16)
        gate = _dot(h, wg_ref[...])
        gate_ref[8 + r0:8 + r0 + sub, :] = gate
        up = _dot(h, wu_ref[...])
        conv = (cb_ref[...] + cw[2:3, :] * gate + cw[1:2, :] * gate_ref[7 + r0:7 + r0 + sub, :]
                + cw[0:1, :] * gate_ref[6 + r0:6 + r0 + sub, :])
        act = (_silu(conv) * up).astype(BF16)
        o_ref[0, r0:r0 + sub, :] = x + _dot(act, wd_ref[...])
    gate_ref[0:8, :] = gate_ref[rows:rows + 8, :]


def _ffn_layer(x, p, rows):
    bsz, seq, _ = x.shape
    grid = (bsz, seq // rows)
    x_spec = pl.BlockSpec((1, rows, D_MODEL), lambda b, j: (b, j, 0))
    consts = [p["g2"], p["w_gate"], p["w_up"], p["conv_w"], p["conv_b"], p["w_down"]]
    return pl.pallas_call(
        functools.partial(_ffn_kernel, rows=rows),
        grid=grid,
        in_specs=[x_spec] + [_const_spec(c) for c in consts],
        out_specs=x_spec,
        out_shape=jax.ShapeDtypeStruct(x.shape, x.dtype),
        scratch_shapes=[pltpu.VMEM((rows + 8, D_FF), F32)],
        compiler_params=pltpu.CompilerParams(
            dimension_semantics=("arbitrary", "arbitrary"), vmem_limit_bytes=VMEM_LIMIT_BYTES),
        name="ffn_layer",
    )(x, *map(_operand, consts))


def _t5_causal_bucket_table():
    qi = np.arange(CHUNK)[:, None]
    kj = np.arange(CHUNK)[None, :]
    prev_side = kj > qi
    dist = np.where(prev_side, qi + CHUNK - kj, qi - kj)
    max_exact = REL_BUCKETS // 2
    nf = np.maximum(dist, 1).astype(np.float64)
    large = max_exact + (np.log(nf / max_exact) / math.log(REL_MAX_DIST / max_exact)
                         * (REL_BUCKETS - max_exact)).astype(np.int32)
    large = np.minimum(large, REL_BUCKETS - 1)
    bucket = np.where(dist < max_exact, dist, large).astype(np.int32)
    return bucket, prev_side


def _attention_bias_tables(rel_bias):
    bucket, prev_side = _t5_causal_bucket_table()
    onehot = (jnp.asarray(bucket.reshape(-1))[:, None] == jnp.arange(REL_BUCKETS)[None, :]).astype(F32)
    bias = jnp.einsum("pb,bh->hp", onehot, rel_bias.astype(F32), precision=lax.Precision.HIGHEST)
    bias = bias.reshape(B_Q_HEADS, CHUNK, CHUNK) * LOG2E
    first = jnp.where(jnp.asarray(prev_side)[None], MASK_VALUE, bias)
    return jnp.stack([first, bias])


def _level_table():
    t = np.arange(CHUNK)[:, None]
    s = np.arange(CHUNK)[None, :]
    x = np.bitwise_xor(t, s)
    msb = np.floor(np.log2(np.maximum(x, 1))).astype(np.int32)
    lvl = np.where(s == t, 0, np.where(s < t, len(HGRN_LEVELS) - msb, -1)).astype(np.int32)
    return np.tile(lvl, (1, C_HEADS))


def _block_diag_ones():
    i = np.arange(256)
    return (i[:, None] // HEAD_DIM == i[None, :] // HEAD_DIM).astype(np.float32)


def kernel(x, norm1_g, w_in, gmlp_vnorm_g, gmlp_w_s, gmlp_b_s, q_norm_g, k_norm_g, attn_sinks, rel_bias,
           hgrn_lb_logits, hgrn_onorm_g, w_out, norm2_g, w_gate, w_up, conv_w, conv_b, w_down):
    depth = w_in.shape[0]
    seq = x.shape[1]
    mixer_rows = min(MIXER_ROWS, seq)
    ffn_rows = min(FFN_ROWS, seq)
    lb_cum = jnp.cumsum(jax.nn.softmax(hgrn_lb_logits.astype(F32), axis=0), axis=0)
    lower_bounds = lb_cum - lb_cum[0]
    bias_tables = _attention_bias_tables(rel_bias)
    causal = jnp.tril(jnp.ones((CHUNK, CHUNK), dtype=bool))
    bd = jnp.asarray(_block_diag_ones(), BF16)
    lvl = jnp.asarray(_level_table())
    w_in, w_out, w_gate, w_up, w_down = (w.astype(BF16) for w in (w_in, w_out, w_gate, w_up, w_down))
    for l in range(depth):
        p = {
            "g1": norm1_g[l].reshape(1, D_MODEL),
            "w_in": _LayerOf(w_in, l),
            "w_out": _LayerOf(w_out, l),
            "bd": bd,
            "a_w": jnp.where(causal, gmlp_w_s[l], 0.0).astype(BF16).reshape(A_GROUPS * CHUNK, CHUNK),
            "a_b": jnp.repeat(gmlp_b_s[l].T, HEAD_DIM, axis=1),
            "a_g": gmlp_vnorm_g[l].reshape(1, A_WIDTH),
            "q_g": jnp.tile(q_norm_g[l], B_Q_HEADS).reshape(1, B_WIDTH) * (HEAD_DIM ** -0.5 * LOG2E),
            "k_g": jnp.tile(k_norm_g[l], B_KV_HEADS).reshape(1, B_KV_WIDTH),
            "bias": bias_tables,
            "sinks": attn_sinks[l].astype(F32),
            "lb": lower_bounds[l].reshape(1, C_WIDTH),
            "o_g": jnp.tile(hgrn_onorm_g[l], C_HEADS).reshape(1, C_WIDTH),
            "lvl": lvl,
            "g2": norm2_g[l].reshape(1, D_MODEL),
            "w_gate": _LayerOf(w_gate, l),
            "w_up": _LayerOf(w_up, l),
            "conv_w": conv_w[l],
            "conv_b": conv_b[l].reshape(1, D_FF),
            "w_down": _LayerOf(w_down, l),
        }
        x = _mixer_layer(x, p, mixer_rows)
        x = _ffn_layer(x, p, ffn_rows)
    return x
```

```python
import collections
import functools
import math
from typing import NamedTuple

import numpy as np
import jax
import jax.numpy as jnp
from jax import lax
from jax.experimental import pallas as pl
from jax.experimental.pallas import tpu as pltpu

F32 = jnp.float32
BF16 = jnp.bfloat16

D_MODEL = 1024
HEAD_DIM = 64
CHUNK = 128
A_GROUPS = 4
A_WIDTH = 256
B_Q_HEADS = 8
B_KV_HEADS = 2
B_GROUP = 4
B_WIDTH = 512
B_KV_WIDTH = 128
REL_BUCKETS = 32
REL_MAX_DIST = 128
C_HEADS = 4
C_WIDTH = 256
IN_WIDTH = 2304
D_FF = 2816
CONV_WIDTH = 3
EPS = 1e-6
MASK_VALUE = -1e30
LOG2E = math.log2(math.e)

PROJ_TILE = 256
PROJ_TILES = IN_WIDTH // PROJ_TILE
TILE_AU, TILE_AV, TILE_BQ, TILE_BKV, TILE_CQ, TILE_CF, TILE_CI, TILE_CG = 0, 1, 2, 4, 5, 6, 7, 8

HGRN_LEVELS = (64, 32, 16, 8, 4, 2, 1)
HGRN_REF_ROW = CHUNK // 2 - 1
HGRN_SAFE_LOG_RANGE = 80.0

MIXER_ROWS = 1024
SUB_ROWS = 256
FFN_ROWS = 512
FFN_SUB_ROWS = 256
VMEM_LIMIT_BYTES = 56 * 1024 * 1024

_dot = functools.partial(jnp.dot, preferred_element_type=F32)
_dot_nt = functools.partial(lax.dot_general, dimension_numbers=(((1,), (1,)), ((), ())),
                            preferred_element_type=F32)


def _split2(x):
    hi = x.astype(BF16)
    mid = (x - hi.astype(F32)).astype(BF16)
    return hi, mid


def _seg_mean_sq(x, bd):
    sq = (x * x).astype(BF16)
    w = x.shape[-1]
    outs = []
    for c0 in range(0, w, 256):
        cw = min(256, w - c0)
        outs.append(_dot(sq[:, c0:c0 + cw], bd[0:cw, 0:cw]))
    out = outs[0] if len(outs) == 1 else jnp.concatenate(outs, axis=-1)
    return out * (1.0 / HEAD_DIM)


def _silu(x):
    hx = 0.5 * x
    return hx + hx * jnp.tanh(hx)


def _gelu(x):
    return 0.5 * x * (1.0 + lax.erf(x * (1.0 / math.sqrt(2.0))))


def _rms_rows(x, g):
    ms = jnp.mean(x * x, axis=-1, keepdims=True)
    return x * lax.rsqrt(ms + EPS) * g


def _gmlp_chunk(u, vb, aw_ref, ab, lane_group):
    r = _dot(aw_ref[...], vb)
    sv = r[0:CHUNK]
    for g in range(1, A_GROUPS):
        sv = jnp.where(lane_group == g, r[g * CHUNK:(g + 1) * CHUNK], sv)
    return u * (sv + ab)


def _attn_chunk(qb, kk, vk, bias_ref, bias_idx, sink_ref, prev_side):
    outs = []
    for g in range(B_KV_HEADS):
        heads = range(g * B_GROUP, (g + 1) * B_GROUP)
        q_stack = jnp.concatenate([qb[:, h * HEAD_DIM:(h + 1) * HEAD_DIM] for h in heads], axis=0)
        s2 = _dot_nt(q_stack, kk[:, g * HEAD_DIM:(g + 1) * HEAD_DIM])
        probs, dens = [], []
        for i, h in enumerate(heads):
            rows = slice(i * CHUNK, (i + 1) * CHUNK)
            s = jnp.where(prev_side, s2[rows, 0:CHUNK], s2[rows, CHUNK:2 * CHUNK]) + bias_ref[bias_idx, h]
            sink = sink_ref[h] * LOG2E
            m = jnp.maximum(jnp.max(s, axis=-1, keepdims=True), sink)
            p = jnp.exp2(s - m)
            dens.append(jnp.sum(p, axis=-1, keepdims=True) + jnp.exp2(sink - m))
            pb = p.astype(BF16)
            zero = jnp.zeros_like(pb)
            probs.append(jnp.concatenate([jnp.where(prev_side, pb, zero), jnp.where(prev_side, zero, pb)], axis=1))
        o = _dot(jnp.concatenate(probs, axis=0), vk[:, g * HEAD_DIM:(g + 1) * HEAD_DIM])
        for i in range(B_GROUP):
            outs.append(o[i * CHUNK:(i + 1) * CHUNK] * (1.0 / dens[i]))
    return jnp.concatenate(outs, axis=-1)


def _head_stack(xb, lane_head):
    zero = jnp.zeros_like(xb)
    return jnp.concatenate([jnp.where(lane_head == h, xb, zero) for h in range(C_HEADS)], axis=0)


def _hgrn_gates(cq, cf, prm, tri):
    log_lb, log_1m_lb, one_m_lb = prm
    qs = _silu(cq)
    e = jnp.exp(-jnp.abs(cf))
    r = 1.0 / (1.0 + e)
    ls = jnp.minimum(cf, 0.0) - jnp.log(1.0 + e)
    kin = one_m_lb * jnp.where(cf > 0.0, e * r, r)
    bq = log_1m_lb + ls
    mx = jnp.maximum(log_lb, bq)
    mn = jnp.minimum(log_lb, bq)
    lf = mx + jnp.log(1.0 + jnp.exp(mn - mx))
    hi, mid = _split2(lf)
    b = _dot(tri, hi) + _dot(tri, mid)
    return qs, kin, b


def _hgrn_scores_one_ref(qs, kin, b, lvl_ref, lane_head):
    d = b - b[HGRN_REF_ROW:HGRN_REF_ROW + 1, :]
    s = _dot_nt((qs * jnp.exp(d)).astype(BF16), _head_stack((kin * jnp.exp(-d)).astype(BF16), lane_head))
    return jnp.where(lvl_ref[...] >= 0, s, 0.0)


def _hgrn_scores_levels(qs, kin, b, lvl_ref, lane_head, row_id):
    lvl = lvl_ref[...]
    a = jnp.where(lvl == 0, _dot_nt(qs.astype(BF16), _head_stack(kin.astype(BF16), lane_head)), 0.0)
    for li, half in enumerate(HGRN_LEVELS):
        blk = 2 * half
        if blk >= 8:
            pieces = []
            for j in range(CHUNK // blk):
                r = j * blk + half - 1
                pieces.append(jnp.broadcast_to(b[r:r + 1, :], (blk, C_WIDTH)))
            bref = pieces[0] if len(pieces) == 1 else jnp.concatenate(pieces, axis=0)
        else:
            pos = row_id % blk
            bref = b
            for p in range(blk):
                shift = p - (half - 1)
                if shift == 0:
                    continue
                bref = jnp.where(pos == p, pltpu.roll(b, shift % CHUNK, axis=0), bref)
        e = jnp.exp(-jnp.abs(b - bref))
        s = _dot_nt((qs * e).astype(BF16), _head_stack((kin * e).astype(BF16), lane_head))
        a = jnp.where(lvl == li + 1, s, a)
    return a


def _hgrn_finish(a, qs, kin, b, ci, cg, st, og, lane_head, bd):
    total = b[CHUNK - 1:CHUNK, :]
    o = _dot(a, _head_stack(ci.astype(BF16), lane_head))
    o = o + _dot_nt((qs * jnp.exp(b)).astype(BF16), st.astype(BF16))
    kdec = (kin * jnp.exp(total - b)).astype(BF16)
    upd = _dot(ci.T.astype(BF16), kdec)
    vrow_head = lax.broadcasted_iota(jnp.int32, (C_WIDTH, C_WIDTH), 0) // HEAD_DIM
    kcol_head = lax.broadcasted_iota(jnp.int32, (C_WIDTH, C_WIDTH), 1) // HEAD_DIM
    st = st * jnp.exp(total) + jnp.where(vrow_head == kcol_head, upd, 0.0)
    o = o * lax.rsqrt(_seg_mean_sq(o, bd) + EPS) * og
    return o * _silu(cg), st


def _mixer_kernel(x_ref, g1_ref, win_ref, wout_ref, bd_ref, aw_ref, ab_ref, ag_ref, qg_ref, kg_ref,
                  bias_ref, sink_ref, lb_ref, og_ref, lvl_ref, o_ref, kv_ref, st_ref, *, rows):
    j = pl.program_id(1)
    n_sub = rows // SUB_ROWS
    sub_chunks = SUB_ROWS // CHUNK

    @pl.when(j == 0)
    def _():
        kv_ref[...] = jnp.zeros_like(kv_ref)
        st_ref[...] = jnp.zeros_like(st_ref)

    bd = bd_ref[...]
    ab = ab_ref[...]
    lane_group = lax.broadcasted_iota(jnp.int32, (CHUNK, A_WIDTH), 1) // HEAD_DIM
    prev_side = (lax.broadcasted_iota(jnp.int32, (CHUNK, CHUNK), 1)
                 > lax.broadcasted_iota(jnp.int32, (CHUNK, CHUNK), 0))
    lb = lb_ref[...]
    prm = (jnp.log(lb), jnp.log1p(-lb), 1.0 - lb)
    tri = (lax.broadcasted_iota(jnp.int32, (CHUNK, CHUNK), 0)
           >= lax.broadcasted_iota(jnp.int32, (CHUNK, CHUNK), 1)).astype(BF16)
    lane_head = lax.broadcasted_iota(jnp.int32, (CHUNK, C_WIDTH), 1) // HEAD_DIM
    og = og_ref[...]
    st_in = st_ref[...]

    n_chunks = n_sub * sub_chunks
    hs = [None] * n_sub
    tiles = [[None] * PROJ_TILES for _ in range(n_sub)]
    ya, yb, gates, scores = ([None] * n_chunks for _ in range(4))
    carry = {"k": kv_ref[:, 0:B_KV_WIDTH], "v": kv_ref[:, B_KV_WIDTH:2 * B_KV_WIDTH],
             "worst": jnp.zeros((1, C_WIDTH), F32)}
    pending = collections.deque()

    def emit_pending(cost, cost_left):
        for _ in range(-(-len(pending) * cost // cost_left)):
            pending.popleft()()

    def load_rows(sb):
        hs[sb] = _rms_rows(x_ref[0, sb * SUB_ROWS:(sb + 1) * SUB_ROWS, :], g1_ref[...]).astype(BF16)

    def project_tile(sb, t):
        tiles[sb][t] = _dot(hs[sb], win_ref[:, t * PROJ_TILE:(t + 1) * PROJ_TILE])

    def tile_rows(sb, t, c):
        return tiles[sb][t][c * CHUNK:(c + 1) * CHUNK]

    def mixer_pieces(sb):
        local = {}

        def prep_a():
            local["u"] = _gelu(tiles[sb][TILE_AU])
            v = _gelu(tiles[sb][TILE_AV])
            local["v"] = (v * lax.rsqrt(_seg_mean_sq(v, bd) + EPS) * ag_ref[...]).astype(BF16)

        def prep_q():
            q = jnp.concatenate([tiles[sb][TILE_BQ], tiles[sb][TILE_BQ + 1]], axis=-1)
            local["q"] = (q * lax.rsqrt(_seg_mean_sq(q, bd) + EPS) * qg_ref[...]).astype(BF16)

        def prep_kv():
            k = tiles[sb][TILE_BKV][:, 0:B_KV_WIDTH]
            local["k"] = (k * lax.rsqrt(_seg_mean_sq(k, bd) + EPS) * kg_ref[...]).astype(BF16)
            local["vv"] = tiles[sb][TILE_BKV][:, B_KV_WIDTH:2 * B_KV_WIDTH].astype(BF16)

        pieces = [(2, prep_a), (1, prep_q), (1, prep_kv)]
        for c in range(sub_chunks):
            i = sb * sub_chunks + c
            sl = slice(c * CHUNK, (c + 1) * CHUNK)

            def gmlp(i=i, sl=sl):
                ya[i] = _gmlp_chunk(local["u"][sl], local["v"][sl], aw_ref, ab, lane_group)

            def attn(i=i, sl=sl):
                kc, vc = local["k"][sl], local["vv"][sl]
                bias_idx = jnp.where(j == 0, 0, 1) if i == 0 else 1
                yb[i] = _attn_chunk(local["q"][sl], jnp.concatenate([carry["k"], kc], axis=0),
                                    jnp.concatenate([carry["v"], vc], axis=0), bias_ref, bias_idx, sink_ref,
                                    prev_side)
                carry["k"], carry["v"] = kc, vc

            def hgrn(i=i, c=c):
                qs, kin, b = _hgrn_gates(tile_rows(sb, TILE_CQ, c), tile_rows(sb, TILE_CF, c), prm, tri)
                gates[i] = (qs, kin, b)
                scores[i] = _hgrn_scores_one_ref(qs, kin, b, lvl_ref, lane_head).astype(BF16)
                m = b[HGRN_REF_ROW:HGRN_REF_ROW + 1, :]
                carry["worst"] = jnp.maximum(carry["worst"], jnp.maximum(-m, m - b[CHUNK - 1:CHUNK, :]))

            pieces += [(1, gmlp), (4, attn), (2, hgrn)]
        return pieces

    def finish_pieces(sb, score_list, state):
        mixed = []

        def chunk_out(c):
            i = sb * sub_chunks + c
            qs, kin, b = gates[i]
            yc, state["st"] = _hgrn_finish(score_list[i], qs, kin, b, tile_rows(sb, TILE_CI, c),
                                           tile_rows(sb, TILE_CG, c), state["st"], og, lane_head, bd)
            mixed.append(jnp.concatenate([ya[i], yb[i], yc], axis=-1).astype(BF16))

        def out_tile(t):
            m = mixed[0] if len(mixed) == 1 else jnp.concatenate(mixed, axis=0)
            cols = slice(t * PROJ_TILE, (t + 1) * PROJ_TILE)
            rows_sb = slice(sb * SUB_ROWS, (sb + 1) * SUB_ROWS)
            o_ref[0, rows_sb, cols] = x_ref[0, rows_sb, cols] + _dot(m, wout_ref[:, cols])

        return ([functools.partial(chunk_out, c) for c in range(sub_chunks)]
                + [functools.partial(out_tile, t) for t in range(D_MODEL // PROJ_TILE)])

    state = {"st": st_in}
    load_rows(0)
    for t in range(PROJ_TILES):
        project_tile(0, t)
    for sb in range(n_sub):
        if sb + 1 < n_sub:
            pending.append(functools.partial(load_rows, sb + 1))
            pending.extend(functools.partial(project_tile, sb + 1, t) for t in range(PROJ_TILES))
        pieces = mixer_pieces(sb)
        cost_left = sum(cost for cost, _ in pieces)
        for cost, piece in pieces:
            piece()
            emit_pending(cost, cost_left)
            cost_left -= cost
        pending.extend(finish_pieces(sb, scores, state))
    while pending:
        pending.popleft()()
    st_ref[...] = state["st"]
    kv_ref[:, 0:B_KV_WIDTH] = carry["k"]
    kv_ref[:, B_KV_WIDTH:2 * B_KV_WIDTH] = carry["v"]

    @pl.when(jnp.max(carry["worst"], axis=-1, keepdims=True)[0, 0] > HGRN_SAFE_LOG_RANGE)
    def _():
        row_id = lax.broadcasted_iota(jnp.int32, (CHUNK, C_WIDTH), 0)
        safe = [_hgrn_scores_levels(qs, kin, b, lvl_ref, lane_head, row_id).astype(BF16) for qs, kin, b in gates]
        redo = {"st": st_in}
        for sb in range(n_sub):
            for piece in finish_pieces(sb, safe, redo):
                piece()
        st_ref[...] = redo["st"]


class _LayerOf(NamedTuple):
    stack: jax.Array
    layer: int


def _operand(c):
    return c.stack if isinstance(c, _LayerOf) else c


def _const_spec(c):
    if isinstance(c, _LayerOf):
        nd = c.stack.ndim - 1
        return pl.BlockSpec((None,) + c.stack.shape[1:], lambda b, j, _l=c.layer, _nd=nd: (_l,) + (0,) * _nd,
                            pipeline_mode=pl.Buffered(1))
    nd = c.ndim
    return pl.BlockSpec(c.shape, lambda b, j, _nd=nd: (0,) * _nd, pipeline_mode=pl.Buffered(1))


def _mixer_layer(x, p, rows):
    bsz, seq, _ = x.shape
    grid = (bsz, seq // rows)
    x_spec = pl.BlockSpec((1, rows, D_MODEL), lambda b, j: (b, j, 0))
    consts = [p["g1"], p["w_in"], p["w_out"], p["bd"], p["a_w"], p["a_b"], p["a_g"], p["q_g"], p["k_g"],
              p["bias"]]
    tail = [p["lb"], p["o_g"], p["lvl"]]
    in_specs = ([x_spec] + [_const_spec(c) for c in consts]
                + [pl.BlockSpec(memory_space=pltpu.SMEM)] + [_const_spec(c) for c in tail])
    return pl.pallas_call(
        functools.partial(_mixer_kernel, rows=rows),
        grid=grid,
        in_specs=in_specs,
        out_specs=x_spec,
        out_shape=jax.ShapeDtypeStruct(x.shape, x.dtype),
        scratch_shapes=[pltpu.VMEM((CHUNK, 2 * B_KV_WIDTH), BF16),
                        pltpu.VMEM((C_WIDTH, C_WIDTH), F32)],
        compiler_params=pltpu.CompilerParams(
            dimension_semantics=("arbitrary", "arbitrary"), vmem_limit_bytes=VMEM_LIMIT_BYTES),
        name="mixer_layer",
    )(x, *map(_operand, consts), p["sinks"], *tail)


def _ffn_kernel(x_ref, g2_ref, wg_ref, wu_ref, cw_ref, cb_ref, wd_ref, o_ref, gate_ref, *, rows):
    j = pl.program_id(1)

    @pl.when(j == 0)
    def _():
        gate_ref[0:8, :] = jnp.zeros((8, D_FF), F32)

    cw = cw_ref[...]
    sub = min(FFN_SUB_ROWS, rows)
    for sb in range(rows // sub):
        r0 = sb * sub
        h = _rms_rows(x_ref[0, r0:r0 + sub, :], g2_ref[...]).astype(BF16)
        gate = _dot(h, wg_ref[...])
        gate_ref[8 + r0:8 + r0 + sub, :] = gate
        up = _dot(h, wu_ref[...])
        conv = (cb_ref[...] + cw[2:3, :] * gate + cw[1:2, :] * gate_ref[7 + r0:7 + r0 + sub, :]
                + cw[0:1, :] * gate_ref[6 + r0:6 + r0 + sub, :])
        act = (_silu(conv) * up).astype(BF16)
        o_ref[0, r0:r0 + sub, :] = x_ref[0, r0:r0 + sub, :] + _dot(act, wd_ref[...])
    gate_ref[0:8, :] = gate_ref[rows:rows + 8, :]


def _ffn_layer(x, p, rows):
    bsz, seq, _ = x.shape
    grid = (bsz, seq // rows)
    x_spec = pl.BlockSpec((1, rows, D_MODEL), lambda b, j: (b, j, 0))
    consts = [p["g2"], p["w_gate"], p["w_up"], p["conv_w"], p["conv_b"], p["w_down"]]
    return pl.pallas_call(
        functools.partial(_ffn_kernel, rows=rows),
        grid=grid,
        in_specs=[x_spec] + [_const_spec(c) for c in consts],
        out_specs=x_spec,
        out_shape=jax.ShapeDtypeStruct(x.shape, x.dtype),
        scratch_shapes=[pltpu.VMEM((rows + 8, D_FF), F32)],
        compiler_params=pltpu.CompilerParams(
            dimension_semantics=("arbitrary", "arbitrary"), vmem_limit_bytes=VMEM_LIMIT_BYTES),
        name="ffn_layer",
    )(x, *map(_operand, consts))


def _t5_causal_bucket_table():
    qi = np.arange(CHUNK)[:, None]
    kj = np.arange(CHUNK)[None, :]
    prev_side = kj > qi
    dist = np.where(prev_side, qi + CHUNK - kj, qi - kj)
    max_exact = REL_BUCKETS // 2
    nf = np.maximum(dist, 1).astype(np.float64)
    large = max_exact + (np.log(nf / max_exact) / math.log(REL_MAX_DIST / max_exact)
                         * (REL_BUCKETS - max_exact)).astype(np.int32)
    large = np.minimum(large, REL_BUCKETS - 1)
    bucket = np.where(dist < max_exact, dist, large).astype(np.int32)
    return bucket, prev_side


def _attention_bias_tables(rel_bias):
    bucket, prev_side = _t5_causal_bucket_table()
    onehot = (jnp.asarray(bucket.reshape(-1))[:, None] == jnp.arange(REL_BUCKETS)[None, :]).astype(F32)
    bias = jnp.einsum("pb,bh->hp", onehot, rel_bias.astype(F32), precision=lax.Precision.HIGHEST)
    bias = bias.reshape(B_Q_HEADS, CHUNK, CHUNK) * LOG2E
    first = jnp.where(jnp.asarray(prev_side)[None], MASK_VALUE, bias)
    return jnp.stack([first, bias])


def _level_table():
    t = np.arange(CHUNK)[:, None]
    s = np.arange(CHUNK)[None, :]
    x = np.bitwise_xor(t, s)
    msb = np.floor(np.log2(np.maximum(x, 1))).astype(np.int32)
    lvl = np.where(s == t, 0, np.where(s < t, len(HGRN_LEVELS) - msb, -1)).astype(np.int32)
    return np.tile(lvl, (1, C_HEADS))


def _block_diag_ones():
    i = np.arange(256)
    return (i[:, None] // HEAD_DIM == i[None, :] // HEAD_DIM).astype(np.float32)


def kernel(x, norm1_g, w_in, gmlp_vnorm_g, gmlp_w_s, gmlp_b_s, q_norm_g, k_norm_g, attn_sinks, rel_bias,
           hgrn_lb_logits, hgrn_onorm_g, w_out, norm2_g, w_gate, w_up, conv_w, conv_b, w_down):
    depth = w_in.shape[0]
    seq = x.shape[1]
    mixer_rows = min(MIXER_ROWS, seq)
    ffn_rows = min(FFN_ROWS, seq)
    lb_cum = jnp.cumsum(jax.nn.softmax(hgrn_lb_logits.astype(F32), axis=0), axis=0)
    lower_bounds = lb_cum - lb_cum[0]
    bias_tables = _attention_bias_tables(rel_bias)
    causal = jnp.tril(jnp.ones((CHUNK, CHUNK), dtype=bool))
    bd = jnp.asarray(_block_diag_ones(), BF16)
    lvl = jnp.asarray(_level_table())
    w_in, w_out, w_gate, w_up, w_down = (w.astype(BF16) for w in (w_in, w_out, w_gate, w_up, w_down))
    for l in range(depth):
        p = {
            "g1": norm1_g[l].reshape(1, D_MODEL),
            "w_in": _LayerOf(w_in, l),
            "w_out": _LayerOf(w_out, l),
            "bd": bd,
            "a_w": jnp.where(causal, gmlp_w_s[l], 0.0).astype(BF16).reshape(A_GROUPS * CHUNK, CHUNK),
            "a_b": jnp.repeat(gmlp_b_s[l].T, HEAD_DIM, axis=1),
            "a_g": gmlp_vnorm_g[l].reshape(1, A_WIDTH),
            "q_g": jnp.tile(q_norm_g[l], B_Q_HEADS).reshape(1, B_WIDTH) * (HEAD_DIM ** -0.5 * LOG2E),
            "k_g": jnp.tile(k_norm_g[l], B_KV_HEADS).reshape(1, B_KV_WIDTH),
            "bias": bias_tables,
            "sinks": attn_sinks[l].astype(F32),
            "lb": lower_bounds[l].reshape(1, C_WIDTH),
            "o_g": jnp.tile(hgrn_onorm_g[l], C_HEADS).reshape(1, C_WIDTH),
            "lvl": lvl,
            "g2": norm2_g[l].reshape(1, D_MODEL),
            "w_gate": _LayerOf(w_gate, l),
            "w_up": _LayerOf(w_up, l),
            "conv_w": conv_w[l],
            "conv_b": conv_b[l].reshape(1, D_FF),
            "w_down": _LayerOf(w_down, l),
        }
        x = _mixer_layer(x, p, mixer_rows)
        x = _ffn_layer(x, p, ffn_rows)
    return x
```

```python
import collections
import functools
import math
from typing import NamedTuple

import numpy as np
import jax
import jax.numpy as jnp
from jax import lax
from jax.experimental import pallas as pl
from jax.experimental.pallas import tpu as pltpu

F32 = jnp.float32
BF16 = jnp.bfloat16

D_MODEL = 1024
HEAD_DIM = 64
CHUNK = 128
A_GROUPS = 4
A_WIDTH = 256
B_Q_HEADS = 8
B_KV_HEADS = 2
B_GROUP = 4
B_WIDTH = 512
B_KV_WIDTH = 128
REL_BUCKETS = 32
REL_MAX_DIST = 128
C_HEADS = 4
C_WIDTH = 256
IN_WIDTH = 2304
D_FF = 2816
CONV_WIDTH = 3
EPS = 1e-6
MASK_VALUE = -1e30
LOG2E = math.log2(math.e)

PROJ_TILE = 256
PROJ_TILES = IN_WIDTH // PROJ_TILE
TILE_AU, TILE_AV, TILE_BQ, TILE_BKV, TILE_CQ, TILE_CF, TILE_CI, TILE_CG = 0, 1, 2, 4, 5, 6, 7, 8

HGRN_LEVELS = (64, 32, 16, 8, 4, 2, 1)
HGRN_REF_ROW = CHUNK // 2 - 1
HGRN_SAFE_LOG_RANGE = 80.0

MIXER_ROWS = 1024
SUB_ROWS = 256
FFN_ROWS = 512
FFN_SUB_ROWS = 256
MIXER_VMEM_LIMIT_BYTES = 50 * 1024 * 1024
FFN_VMEM_LIMIT_BYTES = 32 * 1024 * 1024

_dot = functools.partial(jnp.dot, preferred_element_type=F32)
_dot_nt = functools.partial(lax.dot_general, dimension_numbers=(((1,), (1,)), ((), ())),
                            preferred_element_type=F32)


def _split2(x):
    hi = x.astype(BF16)
    mid = (x - hi.astype(F32)).astype(BF16)
    return hi, mid


def _seg_mean_sq(x, bd):
    sq = (x * x).astype(BF16)
    w = x.shape[-1]
    outs = []
    for c0 in range(0, w, 256):
        cw = min(256, w - c0)
        outs.append(_dot(sq[:, c0:c0 + cw], bd[0:cw, 0:cw]))
    out = outs[0] if len(outs) == 1 else jnp.concatenate(outs, axis=-1)
    return out * (1.0 / HEAD_DIM)


def _silu(x):
    hx = 0.5 * x
    return hx + hx * jnp.tanh(hx)


def _gelu(x):
    return 0.5 * x * (1.0 + lax.erf(x * (1.0 / math.sqrt(2.0))))


def _rms_rows(x, g):
    ms = jnp.mean(x * x, axis=-1, keepdims=True)
    return x * lax.rsqrt(ms + EPS) * g


def _gmlp_chunk(u, vb, aw_ref, ab, lane_group):
    r = _dot(aw_ref[...], vb)
    sv = r[0:CHUNK]
    for g in range(1, A_GROUPS):
        sv = jnp.where(lane_group == g, r[g * CHUNK:(g + 1) * CHUNK], sv)
    return u * (sv + ab)


def _attn_chunk(qb, kk, vk, bias_ref, bias_idx, sink_ref, prev_side):
    outs = []
    for g in range(B_KV_HEADS):
        heads = range(g * B_GROUP, (g + 1) * B_GROUP)
        q_stack = jnp.concatenate([qb[:, h * HEAD_DIM:(h + 1) * HEAD_DIM] for h in heads], axis=0)
        s2 = _dot_nt(q_stack, kk[:, g * HEAD_DIM:(g + 1) * HEAD_DIM])
        probs, dens = [], []
        for i, h in enumerate(heads):
            rows = slice(i * CHUNK, (i + 1) * CHUNK)
            s = jnp.where(prev_side, s2[rows, 0:CHUNK], s2[rows, CHUNK:2 * CHUNK]) + bias_ref[bias_idx, h]
            sink = sink_ref[h] * LOG2E
            m = jnp.maximum(jnp.max(s, axis=-1, keepdims=True), sink)
            p = jnp.exp2(s - m)
            dens.append(jnp.sum(p, axis=-1, keepdims=True) + jnp.exp2(sink - m))
            pb = p.astype(BF16)
            zero = jnp.zeros_like(pb)
            probs.append(jnp.concatenate([jnp.where(prev_side, pb, zero), jnp.where(prev_side, zero, pb)], axis=1))
        o = _dot(jnp.concatenate(probs, axis=0), vk[:, g * HEAD_DIM:(g + 1) * HEAD_DIM])
        for i in range(B_GROUP):
            outs.append(o[i * CHUNK:(i + 1) * CHUNK] * (1.0 / dens[i]))
    return jnp.concatenate(outs, axis=-1)


def _head_stack(xb, lane_head):
    zero = jnp.zeros_like(xb)
    return jnp.concatenate([jnp.where(lane_head == h, xb, zero) for h in range(C_HEADS)], axis=0)


def _hgrn_gates(cq, cf, prm, tri):
    log_lb, log_1m_lb, one_m_lb = prm
    qs = _silu(cq)
    e = jnp.exp(-jnp.abs(cf))
    r = 1.0 / (1.0 + e)
    ls = jnp.minimum(cf, 0.0) - jnp.log(1.0 + e)
    kin = one_m_lb * jnp.where(cf > 0.0, e * r, r)
    bq = log_1m_lb + ls
    mx = jnp.maximum(log_lb, bq)
    mn = jnp.minimum(log_lb, bq)
    lf = mx + jnp.log(1.0 + jnp.exp(mn - mx))
    hi, mid = _split2(lf)
    b = _dot(tri, hi) + _dot(tri, mid)
    return qs, kin, b


def _hgrn_scores_one_ref(qs, kin, b, lvl_ref, lane_head):
    d = b - b[HGRN_REF_ROW:HGRN_REF_ROW + 1, :]
    s = _dot_nt((qs * jnp.exp(d)).astype(BF16), _head_stack((kin * jnp.exp(-d)).astype(BF16), lane_head))
    return jnp.where(lvl_ref[...] >= 0, s, 0.0)


def _hgrn_scores_levels(qs, kin, b, lvl_ref, lane_head, row_id):
    lvl = lvl_ref[...]
    a = jnp.where(lvl == 0, _dot_nt(qs.astype(BF16), _head_stack(kin.astype(BF16), lane_head)), 0.0)
    for li, half in enumerate(HGRN_LEVELS):
        blk = 2 * half
        if blk >= 8:
            pieces = []
            for j in range(CHUNK // blk):
                r = j * blk + half - 1
                pieces.append(jnp.broadcast_to(b[r:r + 1, :], (blk, C_WIDTH)))
            bref = pieces[0] if len(pieces) == 1 else jnp.concatenate(pieces, axis=0)
        else:
            pos = row_id % blk
            bref = b
            for p in range(blk):
                shift = p - (half - 1)
                if shift == 0:
                    continue
                bref = jnp.where(pos == p, pltpu.roll(b, shift % CHUNK, axis=0), bref)
        e = jnp.exp(-jnp.abs(b - bref))
        s = _dot_nt((qs * e).astype(BF16), _head_stack((kin * e).astype(BF16), lane_head))
        a = jnp.where(lvl == li + 1, s, a)
    return a


def _hgrn_finish(a, qs, kin, b, ci, cg, st, og, lane_head, bd):
    total = b[CHUNK - 1:CHUNK, :]
    o = _dot(a, _head_stack(ci.astype(BF16), lane_head))
    o = o + _dot_nt((qs * jnp.exp(b)).astype(BF16), st.astype(BF16))
    kdec = (kin * jnp.exp(total - b)).astype(BF16)
    upd = _dot(ci.T.astype(BF16), kdec)
    vrow_head = lax.broadcasted_iota(jnp.int32, (C_WIDTH, C_WIDTH), 0) // HEAD_DIM
    kcol_head = lax.broadcasted_iota(jnp.int32, (C_WIDTH, C_WIDTH), 1) // HEAD_DIM
    st = st * jnp.exp(total) + jnp.where(vrow_head == kcol_head, upd, 0.0)
    o = o * lax.rsqrt(_seg_mean_sq(o, bd) + EPS) * og
    return o * _silu(cg), st


def _mixer_kernel(x_ref, g1_ref, win_ref, wout_ref, bd_ref, aw_ref, ab_ref, ag_ref, qg_ref, kg_ref,
                  bias_ref, sink_ref, lb_ref, og_ref, lvl_ref, o_ref, kv_ref, st_ref, *, rows):
    j = pl.program_id(1)
    n_sub = rows // SUB_ROWS
    sub_chunks = SUB_ROWS // CHUNK

    @pl.when(j == 0)
    def _():
        kv_ref[...] = jnp.zeros_like(kv_ref)
        st_ref[...] = jnp.zeros_like(st_ref)

    bd = bd_ref[...]
    ab = ab_ref[...]
    lane_group = lax.broadcasted_iota(jnp.int32, (CHUNK, A_WIDTH), 1) // HEAD_DIM
    prev_side = (lax.broadcasted_iota(jnp.int32, (CHUNK, CHUNK), 1)
                 > lax.broadcasted_iota(jnp.int32, (CHUNK, CHUNK), 0))
    lb = lb_ref[...]
    prm = (jnp.log(lb), jnp.log1p(-lb), 1.0 - lb)
    tri = (lax.broadcasted_iota(jnp.int32, (CHUNK, CHUNK), 0)
           >= lax.broadcasted_iota(jnp.int32, (CHUNK, CHUNK), 1)).astype(BF16)
    lane_head = lax.broadcasted_iota(jnp.int32, (CHUNK, C_WIDTH), 1) // HEAD_DIM
    og = og_ref[...]
    st_in = st_ref[...]

    n_chunks = n_sub * sub_chunks
    hs = [None] * n_sub
    tiles = [[None] * PROJ_TILES for _ in range(n_sub)]
    ya, yb, gates, scores = ([None] * n_chunks for _ in range(4))
    carry = {"k": kv_ref[:, 0:B_KV_WIDTH], "v": kv_ref[:, B_KV_WIDTH:2 * B_KV_WIDTH],
             "worst": jnp.zeros((1, C_WIDTH), F32)}
    pending = collections.deque()

    def emit_pending(cost, cost_left):
        for _ in range(-(-len(pending) * cost // cost_left)):
            pending.popleft()()

    def load_rows(sb):
        hs[sb] = _rms_rows(x_ref[0, sb * SUB_ROWS:(sb + 1) * SUB_ROWS, :], g1_ref[...]).astype(BF16)

    def project_tile(sb, t):
        tiles[sb][t] = _dot(hs[sb], win_ref[:, t * PROJ_TILE:(t + 1) * PROJ_TILE])

    def tile_rows(sb, t, c):
        return tiles[sb][t][c * CHUNK:(c + 1) * CHUNK]

    def mixer_pieces(sb):
        local = {}

        def prep_a():
            local["u"] = _gelu(tiles[sb][TILE_AU])
            v = _gelu(tiles[sb][TILE_AV])
            local["v"] = (v * lax.rsqrt(_seg_mean_sq(v, bd) + EPS) * ag_ref[...]).astype(BF16)

        def prep_q():
            q = jnp.concatenate([tiles[sb][TILE_BQ], tiles[sb][TILE_BQ + 1]], axis=-1)
            local["q"] = (q * lax.rsqrt(_seg_mean_sq(q, bd) + EPS) * qg_ref[...]).astype(BF16)

        def prep_kv():
            k = tiles[sb][TILE_BKV][:, 0:B_KV_WIDTH]
            local["k"] = (k * lax.rsqrt(_seg_mean_sq(k, bd) + EPS) * kg_ref[...]).astype(BF16)
            local["vv"] = tiles[sb][TILE_BKV][:, B_KV_WIDTH:2 * B_KV_WIDTH].astype(BF16)

        pieces = [(2, prep_a), (1, prep_q), (1, prep_kv)]
        for c in range(sub_chunks):
            i = sb * sub_chunks + c
            sl = slice(c * CHUNK, (c + 1) * CHUNK)

            def gmlp(i=i, sl=sl):
                ya[i] = _gmlp_chunk(local["u"][sl], local["v"][sl], aw_ref, ab, lane_group)

            def attn(i=i, sl=sl):
                kc, vc = local["k"][sl], local["vv"][sl]
                bias_idx = jnp.where(j == 0, 0, 1) if i == 0 else 1
                yb[i] = _attn_chunk(local["q"][sl], jnp.concatenate([carry["k"], kc], axis=0),
                                    jnp.concatenate([carry["v"], vc], axis=0), bias_ref, bias_idx, sink_ref,
                                    prev_side)
                carry["k"], carry["v"] = kc, vc

            def hgrn(i=i, c=c):
                qs, kin, b = _hgrn_gates(tile_rows(sb, TILE_CQ, c), tile_rows(sb, TILE_CF, c), prm, tri)
                gates[i] = (qs, kin, b)
                scores[i] = _hgrn_scores_one_ref(qs, kin, b, lvl_ref, lane_head).astype(BF16)
                m = b[HGRN_REF_ROW:HGRN_REF_ROW + 1, :]
                carry["worst"] = jnp.maximum(carry["worst"], jnp.maximum(-m, m - b[CHUNK - 1:CHUNK, :]))

            pieces += [(1, gmlp), (4, attn), (2, hgrn)]
        return pieces

    def finish_pieces(sb, score_list, state):
        mixed = []

        def chunk_out(c):
            i = sb * sub_chunks + c
            qs, kin, b = gates[i]
            yc, state["st"] = _hgrn_finish(score_list[i], qs, kin, b, tile_rows(sb, TILE_CI, c),
                                           tile_rows(sb, TILE_CG, c), state["st"], og, lane_head, bd)
            mixed.append(jnp.concatenate([ya[i], yb[i], yc], axis=-1).astype(BF16))

        def out_tile(t):
            m = mixed[0] if len(mixed) == 1 else jnp.concatenate(mixed, axis=0)
            cols = slice(t * PROJ_TILE, (t + 1) * PROJ_TILE)
            rows_sb = slice(sb * SUB_ROWS, (sb + 1) * SUB_ROWS)
            o_ref[0, rows_sb, cols] = x_ref[0, rows_sb, cols] + _dot(m, wout_ref[:, cols])

        return ([functools.partial(chunk_out, c) for c in range(sub_chunks)]
                + [functools.partial(out_tile, t) for t in range(D_MODEL // PROJ_TILE)])

    state = {"st": st_in}
    load_rows(0)
    for t in range(PROJ_TILES):
        project_tile(0, t)
    for sb in range(n_sub):
        if sb + 1 < n_sub:
            pending.append(functools.partial(load_rows, sb + 1))
            pending.extend(functools.partial(project_tile, sb + 1, t) for t in range(PROJ_TILES))
        pieces = mixer_pieces(sb)
        cost_left = sum(cost for cost, _ in pieces)
        for cost, piece in pieces:
            piece()
            emit_pending(cost, cost_left)
            cost_left -= cost
        pending.extend(finish_pieces(sb, scores, state))
    while pending:
        pending.popleft()()
    st_ref[...] = state["st"]
    kv_ref[:, 0:B_KV_WIDTH] = carry["k"]
    kv_ref[:, B_KV_WIDTH:2 * B_KV_WIDTH] = carry["v"]

    @pl.when(jnp.max(carry["worst"], axis=-1, keepdims=True)[0, 0] > HGRN_SAFE_LOG_RANGE)
    def _():
        row_id = lax.broadcasted_iota(jnp.int32, (CHUNK, C_WIDTH), 0)
        safe = [_hgrn_scores_levels(qs, kin, b, lvl_ref, lane_head, row_id).astype(BF16) for qs, kin, b in gates]
        redo = {"st": st_in}
        for sb in range(n_sub):
            for piece in finish_pieces(sb, safe, redo):
                piece()
        st_ref[...] = redo["st"]


class _LayerOf(NamedTuple):
    stack: jax.Array
    layer: int


def _operand(c):
    return c.stack if isinstance(c, _LayerOf) else c


def _const_spec(c):
    if isinstance(c, _LayerOf):
        nd = c.stack.ndim - 1
        return pl.BlockSpec((None,) + c.stack.shape[1:], lambda b, j, _l=c.layer, _nd=nd: (_l,) + (0,) * _nd,
                            pipeline_mode=pl.Buffered(1))
    nd = c.ndim
    return pl.BlockSpec(c.shape, lambda b, j, _nd=nd: (0,) * _nd, pipeline_mode=pl.Buffered(1))


def _mixer_layer(x, p, rows):
    bsz, seq, _ = x.shape
    grid = (bsz, seq // rows)
    x_spec = pl.BlockSpec((1, rows, D_MODEL), lambda b, j: (b, j, 0))
    consts = [p["g1"], p["w_in"], p["w_out"], p["bd"], p["a_w"], p["a_b"], p["a_g"], p["q_g"], p["k_g"],
              p["bias"]]
    tail = [p["lb"], p["o_g"], p["lvl"]]
    in_specs = ([x_spec] + [_const_spec(c) for c in consts]
                + [pl.BlockSpec(memory_space=pltpu.SMEM)] + [_const_spec(c) for c in tail])
    return pl.pallas_call(
        functools.partial(_mixer_kernel, rows=rows),
        grid=grid,
        in_specs=in_specs,
        out_specs=x_spec,
        out_shape=jax.ShapeDtypeStruct(x.shape, x.dtype),
        scratch_shapes=[pltpu.VMEM((CHUNK, 2 * B_KV_WIDTH), BF16),
                        pltpu.VMEM((C_WIDTH, C_WIDTH), F32)],
        compiler_params=pltpu.CompilerParams(
            dimension_semantics=("arbitrary", "arbitrary"), vmem_limit_bytes=MIXER_VMEM_LIMIT_BYTES),
        name="mixer_layer",
    )(x, *map(_operand, consts), p["sinks"], *tail)


def _ffn_kernel(x_ref, g2_ref, wg_ref, wu_ref, cw_ref, cb_ref, wd_ref, o_ref, gate_ref, *, rows):
    j = pl.program_id(1)

    @pl.when(j == 0)
    def _():
        gate_ref[0:8, :] = jnp.zeros((8, D_FF), F32)

    cw = cw_ref[...]
    sub = min(FFN_SUB_ROWS, rows)
    for sb in range(rows // sub):
        r0 = sb * sub
        h = _rms_rows(x_ref[0, r0:r0 + sub, :], g2_ref[...]).astype(BF16)
        gate = _dot(h, wg_ref[...])
        gate_ref[8 + r0:8 + r0 + sub, :] = gate
        up = _dot(h, wu_ref[...])
        conv = (cb_ref[...] + cw[2:3, :] * gate + cw[1:2, :] * gate_ref[7 + r0:7 + r0 + sub, :]
                + cw[0:1, :] * gate_ref[6 + r0:6 + r0 + sub, :])
        act = (_silu(conv) * up).astype(BF16)
        o_ref[0, r0:r0 + sub, :] = x_ref[0, r0:r0 + sub, :] + _dot(act, wd_ref[...])
    gate_ref[0:8, :] = gate_ref[rows:rows + 8, :]


def _ffn_layer(x, p, rows):
    bsz, seq, _ = x.shape
    grid = (bsz, seq // rows)
    x_spec = pl.BlockSpec((1, rows, D_MODEL), lambda b, j: (b, j, 0))
    consts = [p["g2"], p["w_gate"], p["w_up"], p["conv_w"], p["conv_b"], p["w_down"]]
    return pl.pallas_call(
        functools.partial(_ffn_kernel, rows=rows),
        grid=grid,
        in_specs=[x_spec] + [_const_spec(c) for c in consts],
        out_specs=x_spec,
        out_shape=jax.ShapeDtypeStruct(x.shape, x.dtype),
        scratch_shapes=[pltpu.VMEM((rows + 8, D_FF), F32)],
        compiler_params=pltpu.CompilerParams(
            dimension_semantics=("arbitrary", "arbitrary"), vmem_limit_bytes=FFN_VMEM_LIMIT_BYTES),
        name="ffn_layer",
    )(x, *map(_operand, consts))


def _t5_causal_bucket_table():
    qi = np.arange(CHUNK)[:, None]
    kj = np.arange(CHUNK)[None, :]
    prev_side = kj > qi
    dist = np.where(prev_side, qi + CHUNK - kj, qi - kj)
    max_exact = REL_BUCKETS // 2
    nf = np.maximum(dist, 1).astype(np.float64)
    large = max_exact + (np.log(nf / max_exact) / math.log(REL_MAX_DIST / max_exact)
                         * (REL_BUCKETS - max_exact)).astype(np.int32)
    large = np.minimum(large, REL_BUCKETS - 1)
    bucket = np.where(dist < max_exact, dist, large).astype(np.int32)
    return bucket, prev_side


def _attention_bias_tables(rel_bias):
    bucket, prev_side = _t5_causal_bucket_table()
    onehot = (jnp.asarray(bucket.reshape(-1))[:, None] == jnp.arange(REL_BUCKETS)[None, :]).astype(F32)
    bias = jnp.einsum("pb,bh->hp", onehot, rel_bias.astype(F32), precision=lax.Precision.HIGHEST)
    bias = bias.reshape(B_Q_HEADS, CHUNK, CHUNK) * LOG2E
    first = jnp.where(jnp.asarray(prev_side)[None], MASK_VALUE, bias)
    return jnp.stack([first, bias])


def _level_table():
    t = np.arange(CHUNK)[:, None]
    s = np.arange(CHUNK)[None, :]
    x = np.bitwise_xor(t, s)
    msb = np.floor(np.log2(np.maximum(x, 1))).astype(np.int32)
    lvl = np.where(s == t, 0, np.where(s < t, len(HGRN_LEVELS) - msb, -1)).astype(np.int32)
    return np.tile(lvl, (1, C_HEADS))


def _block_diag_ones():
    i = np.arange(256)
    return (i[:, None] // HEAD_DIM == i[None, :] // HEAD_DIM).astype(np.float32)


def kernel(x, norm1_g, w_in, gmlp_vnorm_g, gmlp_w_s, gmlp_b_s, q_norm_g, k_norm_g, attn_sinks, rel_bias,
           hgrn_lb_logits, hgrn_onorm_g, w_out, norm2_g, w_gate, w_up, conv_w, conv_b, w_down):
    depth = w_in.shape[0]
    seq = x.shape[1]
    mixer_rows = min(MIXER_ROWS, seq)
    ffn_rows = min(FFN_ROWS, seq)
    lb_cum = jnp.cumsum(jax.nn.softmax(hgrn_lb_logits.astype(F32), axis=0), axis=0)
    lower_bounds = lb_cum - lb_cum[0]
    bias_tables = _attention_bias_tables(rel_bias)
    causal = jnp.tril(jnp.ones((CHUNK, CHUNK), dtype=bool))
    bd = jnp.asarray(_block_diag_ones(), BF16)
    lvl = jnp.asarray(_level_table())
    w_in, w_out, w_gate, w_up, w_down = (w.astype(BF16) for w in (w_in, w_out, w_gate, w_up, w_down))
    for l in range(depth):
        p = {
            "g1": norm1_g[l].reshape(1, D_MODEL),
            "w_in": _LayerOf(w_in, l),
            "w_out": _LayerOf(w_out, l),
            "bd": bd,
            "a_w": jnp.where(causal, gmlp_w_s[l], 0.0).astype(BF16).reshape(A_GROUPS * CHUNK, CHUNK),
            "a_b": jnp.repeat(gmlp_b_s[l].T, HEAD_DIM, axis=1),
            "a_g": gmlp_vnorm_g[l].reshape(1, A_WIDTH),
            "q_g": jnp.tile(q_norm_g[l], B_Q_HEADS).reshape(1, B_WIDTH) * (HEAD_DIM ** -0.5 * LOG2E),
            "k_g": jnp.tile(k_norm_g[l], B_KV_HEADS).reshape(1, B_KV_WIDTH),
            "bias": bias_tables,
            "sinks": attn_sinks[l].astype(F32),
            "lb": lower_bounds[l].reshape(1, C_WIDTH),
            "o_g": jnp.tile(hgrn_onorm_g[l], C_HEADS).reshape(1, C_WIDTH),
            "lvl": lvl,
            "g2": norm2_g[l].reshape(1, D_MODEL),
            "w_gate": _LayerOf(w_gate, l),
            "w_up": _LayerOf(w_up, l),
            "conv_w": conv_w[l],
            "conv_b": conv_b[l].reshape(1, D_FF),
            "w_down": _LayerOf(w_down, l),
        }
        x = _mixer_layer(x, p, mixer_rows)
        x = _ffn_layer(x, p, ffn_rows)
    return x
```

```python
import collections
import functools
import math
from typing import NamedTuple

import numpy as np
import jax
import jax.numpy as jnp
from jax import lax
from jax.experimental import pallas as pl
from jax.experimental.pallas import tpu as pltpu

F32 = jnp.float32
BF16 = jnp.bfloat16

D_MODEL = 1024
HEAD_DIM = 64
CHUNK = 128
A_GROUPS = 4
A_WIDTH = 256
B_Q_HEADS = 8
B_KV_HEADS = 2
B_GROUP = 4
B_WIDTH = 512
B_KV_WIDTH = 128
REL_BUCKETS = 32
REL_MAX_DIST = 128
C_HEADS = 4
C_WIDTH = 256
IN_WIDTH = 2304
D_FF = 2816
CONV_WIDTH = 3
EPS = 1e-6
MASK_VALUE = -1e30
LOG2E = math.log2(math.e)

PROJ_TILE = 256
PROJ_TILES = IN_WIDTH // PROJ_TILE
TILE_AU, TILE_AV, TILE_BQ, TILE_BKV, TILE_CQ, TILE_CF, TILE_CI, TILE_CG = 0, 1, 2, 4, 5, 6, 7, 8

HGRN_LEVELS = (64, 32, 16, 8, 4, 2, 1)
HGRN_REF_ROW = CHUNK // 2 - 1
HGRN_SAFE_LOG_RANGE = 80.0

MIXER_ROWS = 1024
SUB_ROWS = 256
FFN_ROWS = 512
FFN_SUB_ROWS = 256
VMEM_LIMIT_BYTES = 56 * 1024 * 1024

_dot = functools.partial(jnp.dot, preferred_element_type=F32)
_dot_nt = functools.partial(lax.dot_general, dimension_numbers=(((1,), (1,)), ((), ())),
                            preferred_element_type=F32)


def _split2(x):
    hi = x.astype(BF16)
    mid = (x - hi.astype(F32)).astype(BF16)
    return hi, mid


def _seg_mean_sq(x, bd):
    sq = (x * x).astype(BF16)
    w = x.shape[-1]
    outs = []
    for c0 in range(0, w, 256):
        cw = min(256, w - c0)
        outs.append(_dot(sq[:, c0:c0 + cw], bd[0:cw, 0:cw]))
    out = outs[0] if len(outs) == 1 else jnp.concatenate(outs, axis=-1)
    return out * (1.0 / HEAD_DIM)


def _silu(x):
    hx = 0.5 * x
    return hx + hx * jnp.tanh(hx)


def _gelu(x):
    return 0.5 * x * (1.0 + lax.erf(x * (1.0 / math.sqrt(2.0))))


def _rms_rows(x, g):
    ms = jnp.mean(x * x, axis=-1, keepdims=True)
    return x * lax.rsqrt(ms + EPS) * g


def _gmlp_chunk(u, vb, aw_ref, ab, lane_group):
    r = _dot(aw_ref[...], vb)
    sv = r[0:CHUNK]
    for g in range(1, A_GROUPS):
        sv = jnp.where(lane_group == g, r[g * CHUNK:(g + 1) * CHUNK], sv)
    return u * (sv + ab)


def _attn_chunk(qb, kk, vk, bias_ref, bias_idx, sink_ref, prev_side):
    outs = []
    for g in range(B_KV_HEADS):
        heads = range(g * B_GROUP, (g + 1) * B_GROUP)
        q_stack = jnp.concatenate([qb[:, h * HEAD_DIM:(h + 1) * HEAD_DIM] for h in heads], axis=0)
        s2 = _dot_nt(q_stack, kk[:, g * HEAD_DIM:(g + 1) * HEAD_DIM])
        probs, dens = [], []
        for i, h in enumerate(heads):
            rows = slice(i * CHUNK, (i + 1) * CHUNK)
            s = jnp.where(prev_side, s2[rows, 0:CHUNK], s2[rows, CHUNK:2 * CHUNK]) + bias_ref[bias_idx, h]
            sink = sink_ref[h] * LOG2E
            m = jnp.maximum(jnp.max(s, axis=-1, keepdims=True), sink)
            p = jnp.exp2(s - m)
            dens.append(jnp.sum(p, axis=-1, keepdims=True) + jnp.exp2(sink - m))
            pb = p.astype(BF16)
            zero = jnp.zeros_like(pb)
            probs.append(jnp.concatenate([jnp.where(prev_side, pb, zero), jnp.where(prev_side, zero, pb)], axis=1))
        o = _dot(jnp.concatenate(probs, axis=0), vk[:, g * HEAD_DIM:(g + 1) * HEAD_DIM])
        for i in range(B_GROUP):
            outs.append(o[i * CHUNK:(i + 1) * CHUNK] * (1.0 / dens[i]))
    return jnp.concatenate(outs, axis=-1)


def _head_stack(xb, lane_head):
    zero = jnp.zeros_like(xb)
    return jnp.concatenate([jnp.where(lane_head == h, xb, zero) for h in range(C_HEADS)], axis=0)


def _hgrn_gates(cq, cf, prm, tri):
    log_lb, log_1m_lb, one_m_lb = prm
    qs = _silu(cq)
    e = jnp.exp(-jnp.abs(cf))
    r = 1.0 / (1.0 + e)
    ls = jnp.minimum(cf, 0.0) - jnp.log(1.0 + e)
    kin = one_m_lb * jnp.where(cf > 0.0, e * r, r)
    bq = log_1m_lb + ls
    mx = jnp.maximum(log_lb, bq)
    mn = jnp.minimum(log_lb, bq)
    lf = mx + jnp.log(1.0 + jnp.exp(mn - mx))
    hi, mid = _split2(lf)
    b = _dot(tri, hi) + _dot(tri, mid)
    return qs, kin, b


def _hgrn_scores_one_ref(qs, kin, b, lvl_ref, lane_head):
    d = b - b[HGRN_REF_ROW:HGRN_REF_ROW + 1, :]
    s = _dot_nt((qs * jnp.exp(d)).astype(BF16), _head_stack((kin * jnp.exp(-d)).astype(BF16), lane_head))
    return jnp.where(lvl_ref[...] >= 0, s, 0.0)


def _hgrn_scores_levels(qs, kin, b, lvl_ref, lane_head, row_id):
    lvl = lvl_ref[...]
    a = jnp.where(lvl == 0, _dot_nt(qs.astype(BF16), _head_stack(kin.astype(BF16), lane_head)), 0.0)
    for li, half in enumerate(HGRN_LEVELS):
        blk = 2 * half
        if blk >= 8:
            pieces = []
            for j in range(CHUNK // blk):
                r = j * blk + half - 1
                pieces.append(jnp.broadcast_to(b[r:r + 1, :], (blk, C_WIDTH)))
            bref = pieces[0] if len(pieces) == 1 else jnp.concatenate(pieces, axis=0)
        else:
            pos = row_id % blk
            bref = b
            for p in range(blk):
                shift = p - (half - 1)
                if shift == 0:
                    continue
                bref = jnp.where(pos == p, pltpu.roll(b, shift % CHUNK, axis=0), bref)
        e = jnp.exp(-jnp.abs(b - bref))
        s = _dot_nt((qs * e).astype(BF16), _head_stack((kin * e).astype(BF16), lane_head))
        a = jnp.where(lvl == li + 1, s, a)
    return a


def _hgrn_finish(a, qs, kin, b, ci, cg, st, og, lane_head, bd):
    total = b[CHUNK - 1:CHUNK, :]
    o = _dot(a, _head_stack(ci.astype(BF16), lane_head))
    o = o + _dot_nt((qs * jnp.exp(b)).astype(BF16), st.astype(BF16))
    kdec = (kin * jnp.exp(total - b)).astype(BF16)
    upd = _dot(ci.T.astype(BF16), kdec)
    vrow_head = lax.broadcasted_iota(jnp.int32, (C_WIDTH, C_WIDTH), 0) // HEAD_DIM
    kcol_head = lax.broadcasted_iota(jnp.int32, (C_WIDTH, C_WIDTH), 1) // HEAD_DIM
    st = st * jnp.exp(total) + jnp.where(vrow_head == kcol_head, upd, 0.0)
    o = o * lax.rsqrt(_seg_mean_sq(o, bd) + EPS) * og
    return o * _silu(cg), st


def _mixer_kernel(x_ref, g1_ref, win_ref, wout_ref, bd_ref, aw_ref, ab_ref, ag_ref, qg_ref, kg_ref,
                  bias_ref, sink_ref, lb_ref, og_ref, lvl_ref, o_ref, kv_ref, st_ref, *, rows):
    j = pl.program_id(1)
    n_sub = rows // SUB_ROWS
    sub_chunks = SUB_ROWS // CHUNK

    @pl.when(j == 0)
    def _():
        kv_ref[...] = jnp.zeros_like(kv_ref)
        st_ref[...] = jnp.zeros_like(st_ref)

    bd = bd_ref[...]
    ab = ab_ref[...]
    lane_group = lax.broadcasted_iota(jnp.int32, (CHUNK, A_WIDTH), 1) // HEAD_DIM
    prev_side = (lax.broadcasted_iota(jnp.int32, (CHUNK, CHUNK), 1)
                 > lax.broadcasted_iota(jnp.int32, (CHUNK, CHUNK), 0))
    lb = lb_ref[...]
    prm = (jnp.log(lb), jnp.log1p(-lb), 1.0 - lb)
    tri = (lax.broadcasted_iota(jnp.int32, (CHUNK, CHUNK), 0)
           >= lax.broadcasted_iota(jnp.int32, (CHUNK, CHUNK), 1)).astype(BF16)
    lane_head = lax.broadcasted_iota(jnp.int32, (CHUNK, C_WIDTH), 1) // HEAD_DIM
    og = og_ref[...]
    st_in = st_ref[...]

    n_chunks = n_sub * sub_chunks
    hs = [None] * n_sub
    tiles = [[None] * PROJ_TILES for _ in range(n_sub)]
    ya, yb, gates, scores = ([None] * n_chunks for _ in range(4))
    carry = {"k": kv_ref[:, 0:B_KV_WIDTH], "v": kv_ref[:, B_KV_WIDTH:2 * B_KV_WIDTH],
             "worst": jnp.zeros((1, C_WIDTH), F32)}
    pending = collections.deque()

    def emit_pending(cost, cost_left):
        for _ in range(-(-len(pending) * cost // cost_left)):
            pending.popleft()()

    def load_rows(sb):
        hs[sb] = _rms_rows(x_ref[0, sb * SUB_ROWS:(sb + 1) * SUB_ROWS, :], g1_ref[...]).astype(BF16)

    def project_tile(sb, t):
        tiles[sb][t] = _dot(hs[sb], win_ref[:, t * PROJ_TILE:(t + 1) * PROJ_TILE])

    def tile_rows(sb, t, c):
        return tiles[sb][t][c * CHUNK:(c + 1) * CHUNK]

    def mixer_pieces(sb):
        local = {}

        def prep_a():
            local["u"] = _gelu(tiles[sb][TILE_AU])
            v = _gelu(tiles[sb][TILE_AV])
            local["v"] = (v * lax.rsqrt(_seg_mean_sq(v, bd) + EPS) * ag_ref[...]).astype(BF16)

        def prep_q():
            q = jnp.concatenate([tiles[sb][TILE_BQ], tiles[sb][TILE_BQ + 1]], axis=-1)
            local["q"] = (q * lax.rsqrt(_seg_mean_sq(q, bd) + EPS) * qg_ref[...]).astype(BF16)

        def prep_kv():
            k = tiles[sb][TILE_BKV][:, 0:B_KV_WIDTH]
            local["k"] = (k * lax.rsqrt(_seg_mean_sq(k, bd) + EPS) * kg_ref[...]).astype(BF16)
            local["vv"] = tiles[sb][TILE_BKV][:, B_KV_WIDTH:2 * B_KV_WIDTH].astype(BF16)

        pieces = [(2, prep_a), (1, prep_q), (1, prep_kv)]
        for c in range(sub_chunks):
            i = sb * sub_chunks + c
            sl = slice(c * CHUNK, (c + 1) * CHUNK)

            def gmlp(i=i, sl=sl):
                ya[i] = _gmlp_chunk(local["u"][sl], local["v"][sl], aw_ref, ab, lane_group)

            def attn(i=i, sl=sl):
                kc, vc = local["k"][sl], local["vv"][sl]
                bias_idx = jnp.where(j == 0, 0, 1) if i == 0 else 1
                yb[i] = _attn_chunk(local["q"][sl], jnp.concatenate([carry["k"], kc], axis=0),
                                    jnp.concatenate([carry["v"], vc], axis=0), bias_ref, bias_idx, sink_ref,
                                    prev_side)
                carry["k"], carry["v"] = kc, vc

            def hgrn_gates(i=i, c=c):
                gates[i] = _hgrn_gates(tile_rows(sb, TILE_CQ, c), tile_rows(sb, TILE_CF, c), prm, tri)

            def hgrn_scores(i=i):
                qs, kin, b = gates[i]
                scores[i] = _hgrn_scores_one_ref(qs, kin, b, lvl_ref, lane_head).astype(BF16)
                m = b[HGRN_REF_ROW:HGRN_REF_ROW + 1, :]
                carry["worst"] = jnp.maximum(carry["worst"], jnp.maximum(-m, m - b[CHUNK - 1:CHUNK, :]))

            pieces += [(2, hgrn_gates), (1, gmlp), (4, attn), (2, hgrn_scores)]
        return pieces

    def finish_pieces(sb, score_list, state):
        mixed = []

        def chunk_out(c):
            i = sb * sub_chunks + c
            qs, kin, b = gates[i]
            yc, state["st"] = _hgrn_finish(score_list[i], qs, kin, b, tile_rows(sb, TILE_CI, c),
                                           tile_rows(sb, TILE_CG, c), state["st"], og, lane_head, bd)
            mixed.append(jnp.concatenate([ya[i], yb[i], yc], axis=-1).astype(BF16))

        def out_tile(t):
            m = mixed[0] if len(mixed) == 1 else jnp.concatenate(mixed, axis=0)
            cols = slice(t * PROJ_TILE, (t + 1) * PROJ_TILE)
            rows_sb = slice(sb * SUB_ROWS, (sb + 1) * SUB_ROWS)
            o_ref[0, rows_sb, cols] = x_ref[0, rows_sb, cols] + _dot(m, wout_ref[:, cols])

        return ([functools.partial(chunk_out, c) for c in range(sub_chunks)]
                + [functools.partial(out_tile, t) for t in range(D_MODEL // PROJ_TILE)])

    state = {"st": st_in}
    load_rows(0)
    for t in range(PROJ_TILES):
        project_tile(0, t)
    for sb in range(n_sub):
        if sb + 1 < n_sub:
            pending.append(functools.partial(load_rows, sb + 1))
            pending.extend(functools.partial(project_tile, sb + 1, t) for t in range(PROJ_TILES))
        pieces = mixer_pieces(sb)
        cost_left = sum(cost for cost, _ in pieces)
        for cost, piece in pieces:
            piece()
            emit_pending(cost, cost_left)
            cost_left -= cost
        pending.extend(finish_pieces(sb, scores, state))
    while pending:
        pending.popleft()()
    st_ref[...] = state["st"]
    kv_ref[:, 0:B_KV_WIDTH] = carry["k"]
    kv_ref[:, B_KV_WIDTH:2 * B_KV_WIDTH] = carry["v"]

    @pl.when(jnp.max(carry["worst"], axis=-1, keepdims=True)[0, 0] > HGRN_SAFE_LOG_RANGE)
    def _():
        row_id = lax.broadcasted_iota(jnp.int32, (CHUNK, C_WIDTH), 0)
        safe = [_hgrn_scores_levels(qs, kin, b, lvl_ref, lane_head, row_id).astype(BF16) for qs, kin, b in gates]
        redo = {"st": st_in}
        for sb in range(n_sub):
            for piece in finish_pieces(sb, safe, redo):
                piece()
        st_ref[...] = redo["st"]


class _LayerOf(NamedTuple):
    stack: jax.Array
    layer: int


def _operand(c):
    return c.stack if isinstance(c, _LayerOf) else c


def _const_spec(c):
    if isinstance(c, _LayerOf):
        nd = c.stack.ndim - 1
        return pl.BlockSpec((None,) + c.stack.shape[1:], lambda b, j, _l=c.layer, _nd=nd: (_l,) + (0,) * _nd,
                            pipeline_mode=pl.Buffered(1))
    nd = c.ndim
    return pl.BlockSpec(c.shape, lambda b, j, _nd=nd: (0,) * _nd, pipeline_mode=pl.Buffered(1))


def _mixer_layer(x, p, rows):
    bsz, seq, _ = x.shape
    grid = (bsz, seq // rows)
    x_spec = pl.BlockSpec((1, rows, D_MODEL), lambda b, j: (b, j, 0))
    consts = [p["g1"], p["w_in"], p["w_out"], p["bd"], p["a_w"], p["a_b"], p["a_g"], p["q_g"], p["k_g"],
              p["bias"]]
    tail = [p["lb"], p["o_g"], p["lvl"]]
    in_specs = ([x_spec] + [_const_spec(c) for c in consts]
                + [pl.BlockSpec(memory_space=pltpu.SMEM)] + [_const_spec(c) for c in tail])
    return pl.pallas_call(
        functools.partial(_mixer_kernel, rows=rows),
        grid=grid,
        in_specs=in_specs,
        out_specs=x_spec,
        out_shape=jax.ShapeDtypeStruct(x.shape, x.dtype),
        scratch_shapes=[pltpu.VMEM((CHUNK, 2 * B_KV_WIDTH), BF16),
                        pltpu.VMEM((C_WIDTH, C_WIDTH), F32)],
        compiler_params=pltpu.CompilerParams(
            dimension_semantics=("arbitrary", "arbitrary"), vmem_limit_bytes=VMEM_LIMIT_BYTES),
        name="mixer_layer",
    )(x, *map(_operand, consts), p["sinks"], *tail)


def _ffn_kernel(x_ref, g2_ref, wg_ref, wu_ref, cw_ref, cb_ref, wd_ref, o_ref, gate_ref, *, rows):
    j = pl.program_id(1)

    @pl.when(j == 0)
    def _():
        gate_ref[0:8, :] = jnp.zeros((8, D_FF), F32)

    cw = cw_ref[...]
    sub = min(FFN_SUB_ROWS, rows)
    for sb in range(rows // sub):
        r0 = sb * sub
        h = _rms_rows(x_ref[0, r0:r0 + sub, :], g2_ref[...]).astype(BF16)
        gate = _dot(h, wg_ref[...])
        gate_ref[8 + r0:8 + r0 + sub, :] = gate
        up = _dot(h, wu_ref[...])
        conv = (cb_ref[...] + cw[2:3, :] * gate + cw[1:2, :] * gate_ref[7 + r0:7 + r0 + sub, :]
                + cw[0:1, :] * gate_ref[6 + r0:6 + r0 + sub, :])
        act = (_silu(conv) * up).astype(BF16)
        o_ref[0, r0:r0 + sub, :] = x_ref[0, r0:r0 + sub, :] + _dot(act, wd_ref[...])
    gate_ref[0:8, :] = gate_ref[rows:rows + 8, :]


def _ffn_layer(x, p, rows):
    bsz, seq, _ = x.shape
    grid = (bsz, seq // rows)
    x_spec = pl.BlockSpec((1, rows, D_MODEL), lambda b, j: (b, j, 0))
    consts = [p["g2"], p["w_gate"], p["w_up"], p["conv_w"], p["conv_b"], p["w_down"]]
    return pl.pallas_call(
        functools.partial(_ffn_kernel, rows=rows),
        grid=grid,
        in_specs=[x_spec] + [_const_spec(c) for c in consts],
        out_specs=x_spec,
        out_shape=jax.ShapeDtypeStruct(x.shape, x.dtype),
        scratch_shapes=[pltpu.VMEM((rows + 8, D_FF), F32)],
        compiler_params=pltpu.CompilerParams(
            dimension_semantics=("arbitrary", "arbitrary"), vmem_limit_bytes=VMEM_LIMIT_BYTES),
        name="ffn_layer",
    )(x, *map(_operand, consts))


def _t5_causal_bucket_table():
    qi = np.arange(CHUNK)[:, None]
    kj = np.arange(CHUNK)[None, :]
    prev_side = kj > qi
    dist = np.where(prev_side, qi + CHUNK - kj, qi - kj)
    max_exact = REL_BUCKETS // 2
    nf = np.maximum(dist, 1).astype(np.float64)
    large = max_exact + (np.log(nf / max_exact) / math.log(REL_MAX_DIST / max_exact)
                         * (REL_BUCKETS - max_exact)).astype(np.int32)
    large = np.minimum(large, REL_BUCKETS - 1)
    bucket = np.where(dist < max_exact, dist, large).astype(np.int32)
    return bucket, prev_side


def _attention_bias_tables(rel_bias):
    bucket, prev_side = _t5_causal_bucket_table()
    onehot = (jnp.asarray(bucket.reshape(-1))[:, None] == jnp.arange(REL_BUCKETS)[None, :]).astype(F32)
    bias = jnp.einsum("pb,bh->hp", onehot, rel_bias.astype(F32), precision=lax.Precision.HIGHEST)
    bias = bias.reshape(B_Q_HEADS, CHUNK, CHUNK) * LOG2E
    first = jnp.where(jnp.asarray(prev_side)[None], MASK_VALUE, bias)
    return jnp.stack([first, bias])


def _level_table():
    t = np.arange(CHUNK)[:, None]
    s = np.arange(CHUNK)[None, :]
    x = np.bitwise_xor(t, s)
    msb = np.floor(np.log2(np.maximum(x, 1))).astype(np.int32)
    lvl = np.where(s == t, 0, np.where(s < t, len(HGRN_LEVELS) - msb, -1)).astype(np.int32)
    return np.tile(lvl, (1, C_HEADS))


def _block_diag_ones():
    i = np.arange(256)
    return (i[:, None] // HEAD_DIM == i[None, :] // HEAD_DIM).astype(np.float32)


def kernel(x, norm1_g, w_in, gmlp_vnorm_g, gmlp_w_s, gmlp_b_s, q_norm_g, k_norm_g, attn_sinks, rel_bias,
           hgrn_lb_logits, hgrn_onorm_g, w_out, norm2_g, w_gate, w_up, conv_w, conv_b, w_down):
    depth = w_in.shape[0]
    seq = x.shape[1]
    mixer_rows = min(MIXER_ROWS, seq)
    ffn_rows = min(FFN_ROWS, seq)
    lb_cum = jnp.cumsum(jax.nn.softmax(hgrn_lb_logits.astype(F32), axis=0), axis=0)
    lower_bounds = lb_cum - lb_cum[0]
    bias_tables = _attention_bias_tables(rel_bias)
    causal = jnp.tril(jnp.ones((CHUNK, CHUNK), dtype=bool))
    bd = jnp.asarray(_block_diag_ones(), BF16)
    lvl = jnp.asarray(_level_table())
    w_in, w_out, w_gate, w_up, w_down = (w.astype(BF16) for w in (w_in, w_out, w_gate, w_up, w_down))
    for l in range(depth):
        p = {
            "g1": norm1_g[l].reshape(1, D_MODEL),
            "w_in": _LayerOf(w_in, l),
            "w_out": _LayerOf(w_out, l),
            "bd": bd,
            "a_w": jnp.where(causal, gmlp_w_s[l], 0.0).astype(BF16).reshape(A_GROUPS * CHUNK, CHUNK),
            "a_b": jnp.repeat(gmlp_b_s[l].T, HEAD_DIM, axis=1),
            "a_g": gmlp_vnorm_g[l].reshape(1, A_WIDTH),
            "q_g": jnp.tile(q_norm_g[l], B_Q_HEADS).reshape(1, B_WIDTH) * (HEAD_DIM ** -0.5 * LOG2E),
            "k_g": jnp.tile(k_norm_g[l], B_KV_HEADS).reshape(1, B_KV_WIDTH),
            "bias": bias_tables,
            "sinks": attn_sinks[l].astype(F32),
            "lb": lower_bounds[l].reshape(1, C_WIDTH),
            "o_g": jnp.tile(hgrn_onorm_g[l], C_HEADS).reshape(1, C_WIDTH),
            "lvl": lvl,
            "g2": norm2_g[l].reshape(1, D_MODEL),
            "w_gate": _LayerOf(w_gate, l),
            "w_up": _LayerOf(w_up, l),
            "conv_w": conv_w[l],
            "conv_b": conv_b[l].reshape(1, D_FF),
            "w_down": _LayerOf(w_down, l),
        }
        x = _mixer_layer(x, p, mixer_rows)
        x = _ffn_layer(x, p, ffn_rows)
    return x
```

```python
import collections
import functools
import math
from typing import NamedTuple

import numpy as np
import jax
import jax.numpy as jnp
from jax import lax
from jax.experimental import pallas as pl
from jax.experimental.pallas import tpu as pltpu

F32 = jnp.float32
BF16 = jnp.bfloat16

D_MODEL = 1024
HEAD_DIM = 64
CHUNK = 128
A_GROUPS = 4
A_WIDTH = 256
B_Q_HEADS = 8
B_KV_HEADS = 2
B_GROUP = 4
B_WIDTH = 512
B_KV_WIDTH = 128
REL_BUCKETS = 32
REL_MAX_DIST = 128
C_HEADS = 4
C_WIDTH = 256
IN_WIDTH = 2304
D_FF = 2816
CONV_WIDTH = 3
EPS = 1e-6
MASK_VALUE = -1e30
LOG2E = math.log2(math.e)

PROJ_TILE = 256
PROJ_TILES = IN_WIDTH // PROJ_TILE
TILE_AU, TILE_AV, TILE_BQ, TILE_BKV, TILE_CQ, TILE_CF, TILE_CI, TILE_CG = 0, 1, 2, 4, 5, 6, 7, 8

HGRN_LEVELS = (64, 32, 16, 8, 4, 2, 1)
HGRN_REF_ROW = CHUNK // 2 - 1
HGRN_SAFE_LOG_RANGE = 80.0

MIXER_ROWS = 1024
SUB_ROWS = 256
FFN_ROWS = 512
FFN_SUB_ROWS = 256
VMEM_LIMIT_BYTES = 56 * 1024 * 1024

_dot = functools.partial(jnp.dot, preferred_element_type=F32)
_dot_nt = functools.partial(lax.dot_general, dimension_numbers=(((1,), (1,)), ((), ())),
                            preferred_element_type=F32)


def _split2(x):
    hi = x.astype(BF16)
    mid = (x - hi.astype(F32)).astype(BF16)
    return hi, mid


def _seg_mean_sq(x, bd):
    sq = (x * x).astype(BF16)
    w = x.shape[-1]
    outs = []
    for c0 in range(0, w, 256):
        cw = min(256, w - c0)
        outs.append(_dot(sq[:, c0:c0 + cw], bd[0:cw, 0:cw]))
    out = outs[0] if len(outs) == 1 else jnp.concatenate(outs, axis=-1)
    return out * (1.0 / HEAD_DIM)


def _silu(x):
    hx = 0.5 * x
    return hx + hx * jnp.tanh(hx)


def _gelu(x):
    return 0.5 * x * (1.0 + lax.erf(x * (1.0 / math.sqrt(2.0))))


def _rms_rows(x, g):
    ms = jnp.mean(x * x, axis=-1, keepdims=True)
    return x * lax.rsqrt(ms + EPS) * g


def _gmlp_chunk(u, vb, aw_ref, ab, lane_group):
    r = _dot(aw_ref[...], vb)
    sv = r[0:CHUNK]
    for g in range(1, A_GROUPS):
        sv = jnp.where(lane_group == g, r[g * CHUNK:(g + 1) * CHUNK], sv)
    return u * (sv + ab)


def _attn_chunk(qb, kk, vk, bias_ref, bias_idx, sink_ref, prev_side):
    outs = []
    for g in range(B_KV_HEADS):
        heads = range(g * B_GROUP, (g + 1) * B_GROUP)
        q_stack = jnp.concatenate([qb[:, h * HEAD_DIM:(h + 1) * HEAD_DIM] for h in heads], axis=0)
        s2 = _dot_nt(q_stack, kk[:, g * HEAD_DIM:(g + 1) * HEAD_DIM])
        probs, dens = [], []
        for i, h in enumerate(heads):
            rows = slice(i * CHUNK, (i + 1) * CHUNK)
            s = jnp.where(prev_side, s2[rows, 0:CHUNK], s2[rows, CHUNK:2 * CHUNK]) + bias_ref[bias_idx, h]
            sink = sink_ref[h] * LOG2E
            m = jnp.maximum(jnp.max(s, axis=-1, keepdims=True), sink)
            p = jnp.exp2(s - m)
            dens.append(jnp.sum(p, axis=-1, keepdims=True) + jnp.exp2(sink - m))
            pb = p.astype(BF16)
            zero = jnp.zeros_like(pb)
            probs.append(jnp.concatenate([jnp.where(prev_side, pb, zero), jnp.where(prev_side, zero, pb)], axis=1))
        o = _dot(jnp.concatenate(probs, axis=0), vk[:, g * HEAD_DIM:(g + 1) * HEAD_DIM])
        for i in range(B_GROUP):
            outs.append(o[i * CHUNK:(i + 1) * CHUNK] * (1.0 / dens[i]))
    return jnp.concatenate(outs, axis=-1)


def _head_stack(xb, lane_head):
    zero = jnp.zeros_like(xb)
    return jnp.concatenate([jnp.where(lane_head == h, xb, zero) for h in range(C_HEADS)], axis=0)


def _hgrn_gates(cq, cf, prm, tri):
    log_lb, log_1m_lb, one_m_lb = prm
    qs = _silu(cq)
    e = jnp.exp(-jnp.abs(cf))
    r = 1.0 / (1.0 + e)
    ls = jnp.minimum(cf, 0.0) - jnp.log(1.0 + e)
    kin = one_m_lb * jnp.where(cf > 0.0, e * r, r)
    bq = log_1m_lb + ls
    mx = jnp.maximum(log_lb, bq)
    mn = jnp.minimum(log_lb, bq)
    lf = mx + jnp.log(1.0 + jnp.exp(mn - mx))
    hi, mid = _split2(lf)
    b = _dot(tri, hi) + _dot(tri, mid)
    return qs, kin, b


def _hgrn_scores_one_ref(qs, kin, b, lvl_ref, lane_head):
    d = b - b[HGRN_REF_ROW:HGRN_REF_ROW + 1, :]
    s = _dot_nt((qs * jnp.exp(d)).astype(BF16), _head_stack((kin * jnp.exp(-d)).astype(BF16), lane_head))
    return jnp.where(lvl_ref[...] >= 0, s, 0.0)


def _hgrn_scores_levels(qs, kin, b, lvl_ref, lane_head, row_id):
    lvl = lvl_ref[...]
    a = jnp.where(lvl == 0, _dot_nt(qs.astype(BF16), _head_stack(kin.astype(BF16), lane_head)), 0.0)
    for li, half in enumerate(HGRN_LEVELS):
        blk = 2 * half
        if blk >= 8:
            pieces = []
            for j in range(CHUNK // blk):
                r = j * blk + half - 1
                pieces.append(jnp.broadcast_to(b[r:r + 1, :], (blk, C_WIDTH)))
            bref = pieces[0] if len(pieces) == 1 else jnp.concatenate(pieces, axis=0)
        else:
            pos = row_id % blk
            bref = b
            for p in range(blk):
                shift = p - (half - 1)
                if shift == 0:
                    continue
                bref = jnp.where(pos == p, pltpu.roll(b, shift % CHUNK, axis=0), bref)
        e = jnp.exp(-jnp.abs(b - bref))
        s = _dot_nt((qs * e).astype(BF16), _head_stack((kin * e).astype(BF16), lane_head))
        a = jnp.where(lvl == li + 1, s, a)
    return a


def _hgrn_finish(a, qs, kin, b, ci, cg, st, og, lane_head, bd):
    total = b[CHUNK - 1:CHUNK, :]
    o = _dot(a, _head_stack(ci.astype(BF16), lane_head))
    o = o + _dot_nt((qs * jnp.exp(b)).astype(BF16), st.astype(BF16))
    kdec = (kin * jnp.exp(total - b)).astype(BF16)
    upd = _dot(ci.T.astype(BF16), kdec)
    vrow_head = lax.broadcasted_iota(jnp.int32, (C_WIDTH, C_WIDTH), 0) // HEAD_DIM
    kcol_head = lax.broadcasted_iota(jnp.int32, (C_WIDTH, C_WIDTH), 1) // HEAD_DIM
    st = st * jnp.exp(total) + jnp.where(vrow_head == kcol_head, upd, 0.0)
    o = o * lax.rsqrt(_seg_mean_sq(o, bd) + EPS) * og
    return o * _silu(cg), st


def _mixer_kernel(x_ref, g1_ref, win_ref, wout_ref, bd_ref, aw_ref, ab_ref, ag_ref, qg_ref, kg_ref,
                  bias_ref, sink_ref, lb_ref, og_ref, lvl_ref, o_ref, kv_ref, st_ref, *, rows):
    j = pl.program_id(1)
    n_sub = rows // SUB_ROWS
    sub_chunks = SUB_ROWS // CHUNK

    @pl.when(j == 0)
    def _():
        kv_ref[...] = jnp.zeros_like(kv_ref)
        st_ref[...] = jnp.zeros_like(st_ref)

    bd = bd_ref[...]
    ab = ab_ref[...]
    lane_group = lax.broadcasted_iota(jnp.int32, (CHUNK, A_WIDTH), 1) // HEAD_DIM
    prev_side = (lax.broadcasted_iota(jnp.int32, (CHUNK, CHUNK), 1)
                 > lax.broadcasted_iota(jnp.int32, (CHUNK, CHUNK), 0))
    lb = lb_ref[...]
    prm = (jnp.log(lb), jnp.log1p(-lb), 1.0 - lb)
    tri = (lax.broadcasted_iota(jnp.int32, (CHUNK, CHUNK), 0)
           >= lax.broadcasted_iota(jnp.int32, (CHUNK, CHUNK), 1)).astype(BF16)
    lane_head = lax.broadcasted_iota(jnp.int32, (CHUNK, C_WIDTH), 1) // HEAD_DIM
    og = og_ref[...]
    st_in = st_ref[...]

    n_chunks = n_sub * sub_chunks
    hs = [None] * n_sub
    tiles = [[None] * PROJ_TILES for _ in range(n_sub)]
    ya, yb, gates, scores = ([None] * n_chunks for _ in range(4))
    carry = {"k": kv_ref[:, 0:B_KV_WIDTH], "v": kv_ref[:, B_KV_WIDTH:2 * B_KV_WIDTH],
             "worst": jnp.zeros((1, C_WIDTH), F32)}
    pending = collections.deque()

    def emit_pending(cost, cost_left):
        for _ in range(-(-len(pending) * cost // cost_left)):
            pending.popleft()()

    def queue_alternating(new):
        old = list(pending)
        pending.clear()
        for k in range(max(len(old), len(new))):
            pending.extend(new[k:k + 1] + old[k:k + 1])

    def load_rows(sb):
        hs[sb] = _rms_rows(x_ref[0, sb * SUB_ROWS:(sb + 1) * SUB_ROWS, :], g1_ref[...]).astype(BF16)

    def project_tile(sb, t):
        tiles[sb][t] = _dot(hs[sb], win_ref[:, t * PROJ_TILE:(t + 1) * PROJ_TILE])

    def tile_rows(sb, t, c):
        return tiles[sb][t][c * CHUNK:(c + 1) * CHUNK]

    def mixer_pieces(sb):
        local = {}

        def prep_a():
            local["u"] = _gelu(tiles[sb][TILE_AU])
            v = _gelu(tiles[sb][TILE_AV])
            local["v"] = (v * lax.rsqrt(_seg_mean_sq(v, bd) + EPS) * ag_ref[...]).astype(BF16)

        def prep_q():
            q = jnp.concatenate([tiles[sb][TILE_BQ], tiles[sb][TILE_BQ + 1]], axis=-1)
            local["q"] = (q * lax.rsqrt(_seg_mean_sq(q, bd) + EPS) * qg_ref[...]).astype(BF16)

        def prep_kv():
            k = tiles[sb][TILE_BKV][:, 0:B_KV_WIDTH]
            local["k"] = (k * lax.rsqrt(_seg_mean_sq(k, bd) + EPS) * kg_ref[...]).astype(BF16)
            local["vv"] = tiles[sb][TILE_BKV][:, B_KV_WIDTH:2 * B_KV_WIDTH].astype(BF16)

        pieces = [(2, prep_a), (1, prep_q), (1, prep_kv)]
        per_chunk = []
        for c in range(sub_chunks):
            i = sb * sub_chunks + c
            sl = slice(c * CHUNK, (c + 1) * CHUNK)

            def gmlp(i=i, sl=sl):
                ya[i] = _gmlp_chunk(local["u"][sl], local["v"][sl], aw_ref, ab, lane_group)

            def attn(i=i, sl=sl):
                kc, vc = local["k"][sl], local["vv"][sl]
                bias_idx = jnp.where(j == 0, 0, 1) if i == 0 else 1
                yb[i] = _attn_chunk(local["q"][sl], jnp.concatenate([carry["k"], kc], axis=0),
                                    jnp.concatenate([carry["v"], vc], axis=0), bias_ref, bias_idx, sink_ref,
                                    prev_side)
                carry["k"], carry["v"] = kc, vc

            def hgrn_gates(i=i, c=c):
                gates[i] = _hgrn_gates(tile_rows(sb, TILE_CQ, c), tile_rows(sb, TILE_CF, c), prm, tri)

            def hgrn_scores(i=i):
                qs, kin, b = gates[i]
                scores[i] = _hgrn_scores_one_ref(qs, kin, b, lvl_ref, lane_head).astype(BF16)
                m = b[HGRN_REF_ROW:HGRN_REF_ROW + 1, :]
                carry["worst"] = jnp.maximum(carry["worst"], jnp.maximum(-m, m - b[CHUNK - 1:CHUNK, :]))

            per_chunk.append((hgrn_gates, gmlp, attn, hgrn_scores))
        pieces.append((2, per_chunk[0][0]))
        for c, (_, gmlp, attn, hgrn_scores) in enumerate(per_chunk):
            pieces += [(1, gmlp), (4, attn)]
            if c + 1 < sub_chunks:
                pieces.append((2, per_chunk[c + 1][0]))
            pieces.append((2, hgrn_scores))
        return pieces

    def finish_pieces(sb, score_list, state):
        mixed = []

        def chunk_out(c):
            i = sb * sub_chunks + c
            qs, kin, b = gates[i]
            yc, state["st"] = _hgrn_finish(score_list[i], qs, kin, b, tile_rows(sb, TILE_CI, c),
                                           tile_rows(sb, TILE_CG, c), state["st"], og, lane_head, bd)
            mixed.append(jnp.concatenate([ya[i], yb[i], yc], axis=-1).astype(BF16))

        def out_tile(t):
            m = mixed[0] if len(mixed) == 1 else jnp.concatenate(mixed, axis=0)
            cols = slice(t * PROJ_TILE, (t + 1) * PROJ_TILE)
            rows_sb = slice(sb * SUB_ROWS, (sb + 1) * SUB_ROWS)
            o_ref[0, rows_sb, cols] = x_ref[0, rows_sb, cols] + _dot(m, wout_ref[:, cols])

        return ([functools.partial(chunk_out, c) for c in range(sub_chunks)]
                + [functools.partial(out_tile, t) for t in range(D_MODEL // PROJ_TILE)])

    state = {"st": st_in}
    load_rows(0)
    for t in range(PROJ_TILES):
        project_tile(0, t)
    for sb in range(n_sub):
        if sb + 1 < n_sub:
            queue_alternating([functools.partial(load_rows, sb + 1)]
                              + [functools.partial(project_tile, sb + 1, t) for t in range(PROJ_TILES)])
        pieces = mixer_pieces(sb)
        cost_left = sum(cost for cost, _ in pieces)
        for cost, piece in pieces:
            piece()
            emit_pending(cost, cost_left)
            cost_left -= cost
        pending.extend(finish_pieces(sb, scores, state))
    while pending:
        pending.popleft()()
    st_ref[...] = state["st"]
    kv_ref[:, 0:B_KV_WIDTH] = carry["k"]
    kv_ref[:, B_KV_WIDTH:2 * B_KV_WIDTH] = carry["v"]

    @pl.when(jnp.max(carry["worst"], axis=-1, keepdims=True)[0, 0] > HGRN_SAFE_LOG_RANGE)
    def _():
        row_id = lax.broadcasted_iota(jnp.int32, (CHUNK, C_WIDTH), 0)
        safe = [_hgrn_scores_levels(qs, kin, b, lvl_ref, lane_head, row_id).astype(BF16) for qs, kin, b in gates]
        redo = {"st": st_in}
        for sb in range(n_sub):
            for piece in finish_pieces(sb, safe, redo):
                piece()
        st_ref[...] = redo["st"]


class _LayerOf(NamedTuple):
    stack: jax.Array
    layer: int


def _operand(c):
    return c.stack if isinstance(c, _LayerOf) else c


def _const_spec(c):
    if isinstance(c, _LayerOf):
        nd = c.stack.ndim - 1
        return pl.BlockSpec((None,) + c.stack.shape[1:], lambda b, j, _l=c.layer, _nd=nd: (_l,) + (0,) * _nd,
                            pipeline_mode=pl.Buffered(1))
    nd = c.ndim
    return pl.BlockSpec(c.shape, lambda b, j, _nd=nd: (0,) * _nd, pipeline_mode=pl.Buffered(1))


def _mixer_layer(x, p, rows):
    bsz, seq, _ = x.shape
    grid = (bsz, seq // rows)
    x_spec = pl.BlockSpec((1, rows, D_MODEL), lambda b, j: (b, j, 0))
    consts = [p["g1"], p["w_in"], p["w_out"], p["bd"], p["a_w"], p["a_b"], p["a_g"], p["q_g"], p["k_g"],
              p["bias"]]
    tail = [p["lb"], p["o_g"], p["lvl"]]
    in_specs = ([x_spec] + [_const_spec(c) for c in consts]
                + [pl.BlockSpec(memory_space=pltpu.SMEM)] + [_const_spec(c) for c in tail])
    return pl.pallas_call(
        functools.partial(_mixer_kernel, rows=rows),
        grid=grid,
        in_specs=in_specs,
        out_specs=x_spec,
        out_shape=jax.ShapeDtypeStruct(x.shape, x.dtype),
        scratch_shapes=[pltpu.VMEM((CHUNK, 2 * B_KV_WIDTH), BF16),
                        pltpu.VMEM((C_WIDTH, C_WIDTH), F32)],
        compiler_params=pltpu.CompilerParams(
            dimension_semantics=("arbitrary", "arbitrary"), vmem_limit_bytes=VMEM_LIMIT_BYTES),
        name="mixer_layer",
    )(x, *map(_operand, consts), p["sinks"], *tail)


def _ffn_kernel(x_ref, g2_ref, wg_ref, wu_ref, cw_ref, cb_ref, wd_ref, o_ref, gate_ref, *, rows):
    j = pl.program_id(1)

    @pl.when(j == 0)
    def _():
        gate_ref[0:8, :] = jnp.zeros((8, D_FF), F32)

    cw = cw_ref[...]
    sub = min(FFN_SUB_ROWS, rows)
    for sb in range(rows // sub):
        r0 = sb * sub
        h = _rms_rows(x_ref[0, r0:r0 + sub, :], g2_ref[...]).astype(BF16)
        gate = _dot(h, wg_ref[...])
        gate_ref[8 + r0:8 + r0 + sub, :] = gate
        up = _dot(h, wu_ref[...])
        conv = (cb_ref[...] + cw[2:3, :] * gate + cw[1:2, :] * gate_ref[7 + r0:7 + r0 + sub, :]
                + cw[0:1, :] * gate_ref[6 + r0:6 + r0 + sub, :])
        act = (_silu(conv) * up).astype(BF16)
        o_ref[0, r0:r0 + sub, :] = x_ref[0, r0:r0 + sub, :] + _dot(act, wd_ref[...])
    gate_ref[0:8, :] = gate_ref[rows:rows + 8, :]


def _ffn_layer(x, p, rows):
    bsz, seq, _ = x.shape
    grid = (bsz, seq // rows)
    x_spec = pl.BlockSpec((1, rows, D_MODEL), lambda b, j: (b, j, 0))
    consts = [p["g2"], p["w_gate"], p["w_up"], p["conv_w"], p["conv_b"], p["w_down"]]
    return pl.pallas_call(
        functools.partial(_ffn_kernel, rows=rows),
        grid=grid,
        in_specs=[x_spec] + [_const_spec(c) for c in consts],
        out_specs=x_spec,
        out_shape=jax.ShapeDtypeStruct(x.shape, x.dtype),
        scratch_shapes=[pltpu.VMEM((rows + 8, D_FF), F32)],
        compiler_params=pltpu.CompilerParams(
            dimension_semantics=("arbitrary", "arbitrary"), vmem_limit_bytes=VMEM_LIMIT_BYTES),
        name="ffn_layer",
    )(x, *map(_operand, consts))


def _t5_causal_bucket_table():
    qi = np.arange(CHUNK)[:, None]
    kj = np.arange(CHUNK)[None, :]
    prev_side = kj > qi
    dist = np.where(prev_side, qi + CHUNK - kj, qi - kj)
    max_exact = REL_BUCKETS // 2
    nf = np.maximum(dist, 1).astype(np.float64)
    large = max_exact + (np.log(nf / max_exact) / math.log(REL_MAX_DIST / max_exact)
                         * (REL_BUCKETS - max_exact)).astype(np.int32)
    large = np.minimum(large, REL_BUCKETS - 1)
    bucket = np.where(dist < max_exact, dist, large).astype(np.int32)
    return bucket, prev_side


def _attention_bias_tables(rel_bias):
    bucket, prev_side = _t5_causal_bucket_table()
    onehot = (jnp.asarray(bucket.reshape(-1))[:, None] == jnp.arange(REL_BUCKETS)[None, :]).astype(F32)
    bias = jnp.einsum("pb,bh->hp", onehot, rel_bias.astype(F32), precision=lax.Precision.HIGHEST)
    bias = bias.reshape(B_Q_HEADS, CHUNK, CHUNK) * LOG2E
    first = jnp.where(jnp.asarray(prev_side)[None], MASK_VALUE, bias)
    return jnp.stack([first, bias])


def _level_table():
    t = np.arange(CHUNK)[:, None]
    s = np.arange(CHUNK)[None, :]
    x = np.bitwise_xor(t, s)
    msb = np.floor(np.log2(np.maximum(x, 1))).astype(np.int32)
    lvl = np.where(s == t, 0, np.where(s < t, len(HGRN_LEVELS) - msb, -1)).astype(np.int32)
    return np.tile(lvl, (1, C_HEADS))


def _block_diag_ones():
    i = np.arange(256)
    return (i[:, None] // HEAD_DIM == i[None, :] // HEAD_DIM).astype(np.float32)


def kernel(x, norm1_g, w_in, gmlp_vnorm_g, gmlp_w_s, gmlp_b_s, q_norm_g, k_norm_g, attn_sinks, rel_bias,
           hgrn_lb_logits, hgrn_onorm_g, w_out, norm2_g, w_gate, w_up, conv_w, conv_b, w_down):
    depth = w_in.shape[0]
    seq = x.shape[1]
    mixer_rows = min(MIXER_ROWS, seq)
    ffn_rows = min(FFN_ROWS, seq)
    lb_cum = jnp.cumsum(jax.nn.softmax(hgrn_lb_logits.astype(F32), axis=0), axis=0)
    lower_bounds = lb_cum - lb_cum[0]
    bias_tables = _attention_bias_tables(rel_bias)
    causal = jnp.tril(jnp.ones((CHUNK, CHUNK), dtype=bool))
    bd = jnp.asarray(_block_diag_ones(), BF16)
    lvl = jnp.asarray(_level_table())
    w_in, w_out, w_gate, w_up, w_down = (w.astype(BF16) for w in (w_in, w_out, w_gate, w_up, w_down))
    for l in range(depth):
        p = {
            "g1": norm1_g[l].reshape(1, D_MODEL),
            "w_in": _LayerOf(w_in, l),
            "w_out": _LayerOf(w_out, l),
            "bd": bd,
            "a_w": jnp.where(causal, gmlp_w_s[l], 0.0).astype(BF16).reshape(A_GROUPS * CHUNK, CHUNK),
            "a_b": jnp.repeat(gmlp_b_s[l].T, HEAD_DIM, axis=1),
            "a_g": gmlp_vnorm_g[l].reshape(1, A_WIDTH),
            "q_g": jnp.tile(q_norm_g[l], B_Q_HEADS).reshape(1, B_WIDTH) * (HEAD_DIM ** -0.5 * LOG2E),
            "k_g": jnp.tile(k_norm_g[l], B_KV_HEADS).reshape(1, B_KV_WIDTH),
            "bias": bias_tables,
            "sinks": attn_sinks[l].astype(F32),
            "lb": lower_bounds[l].reshape(1, C_WIDTH),
            "o_g": jnp.tile(hgrn_onorm_g[l], C_HEADS).reshape(1, C_WIDTH),
            "lvl": lvl,
            "g2": norm2_g[l].reshape(1, D_MODEL),
            "w_gate": _LayerOf(w_gate, l),
            "w_up": _LayerOf(w_up, l),
            "conv_w": conv_w[l],
            "conv_b": conv_b[l].reshape(1, D_FF),
            "w_down": _LayerOf(w_down, l),
        }
        x = _mixer_layer(x, p, mixer_rows)
        x = _ffn_layer(x, p, ffn_rows)
    return x
```

```python
import collections
import functools
import math
from typing import NamedTuple

import numpy as np
import jax
import jax.numpy as jnp
from jax import lax
from jax.experimental import pallas as pl
from jax.experimental.pallas import tpu as pltpu

F32 = jnp.float32
BF16 = jnp.bfloat16

D_MODEL = 1024
HEAD_DIM = 64
CHUNK = 128
A_GROUPS = 4
A_WIDTH = 256
B_Q_HEADS = 8
B_KV_HEADS = 2
B_GROUP = 4
B_WIDTH = 512
B_KV_WIDTH = 128
REL_BUCKETS = 32
REL_MAX_DIST = 128
C_HEADS = 4
C_WIDTH = 256
IN_WIDTH = 2304
D_FF = 2816
CONV_WIDTH = 3
EPS = 1e-6
MASK_VALUE = -1e30
LOG2E = math.log2(math.e)

PROJ_TILE = 256
PROJ_TILES = IN_WIDTH // PROJ_TILE
TILE_AU, TILE_AV, TILE_BQ, TILE_BKV, TILE_CQ, TILE_CF, TILE_CI, TILE_CG = 0, 1, 2, 4, 5, 6, 7, 8

HGRN_LEVELS = (64, 32, 16, 8, 4, 2, 1)
HGRN_REF_ROW = CHUNK // 2 - 1
HGRN_SAFE_LOG_RANGE = 80.0

MIXER_ROWS = 1024
SUB_ROWS = 256
FFN_ROWS = 512
FFN_SUB_ROWS = 256
VMEM_LIMIT_BYTES = 56 * 1024 * 1024

_dot = functools.partial(jnp.dot, preferred_element_type=F32)
_dot_nt = functools.partial(lax.dot_general, dimension_numbers=(((1,), (1,)), ((), ())),
                            preferred_element_type=F32)


def _split2(x):
    hi = x.astype(BF16)
    mid = (x - hi.astype(F32)).astype(BF16)
    return hi, mid


def _seg_mean_sq(x, bd):
    sq = (x * x).astype(BF16)
    w = x.shape[-1]
    outs = []
    for c0 in range(0, w, 256):
        cw = min(256, w - c0)
        outs.append(_dot(sq[:, c0:c0 + cw], bd[0:cw, 0:cw]))
    out = outs[0] if len(outs) == 1 else jnp.concatenate(outs, axis=-1)
    return out * (1.0 / HEAD_DIM)


def _silu(x):
    hx = 0.5 * x
    return hx + hx * jnp.tanh(hx)


def _gelu(x):
    return 0.5 * x * (1.0 + lax.erf(x * (1.0 / math.sqrt(2.0))))


def _rms_rows(x, g):
    ms = jnp.mean(x * x, axis=-1, keepdims=True)
    return x * lax.rsqrt(ms + EPS) * g


def _gmlp_chunk(u, vb, aw_ref, ab, lane_group):
    r = _dot(aw_ref[...], vb)
    sv = r[0:CHUNK]
    for g in range(1, A_GROUPS):
        sv = jnp.where(lane_group == g, r[g * CHUNK:(g + 1) * CHUNK], sv)
    return u * (sv + ab)


def _attn_chunk(qb, kk, vk, bias_ref, bias_idx, sink_ref, prev_side):
    outs = []
    for g in range(B_KV_HEADS):
        heads = range(g * B_GROUP, (g + 1) * B_GROUP)
        q_stack = jnp.concatenate([qb[:, h * HEAD_DIM:(h + 1) * HEAD_DIM] for h in heads], axis=0)
        s2 = _dot_nt(q_stack, kk[:, g * HEAD_DIM:(g + 1) * HEAD_DIM])
        probs, dens = [], []
        for i, h in enumerate(heads):
            rows = slice(i * CHUNK, (i + 1) * CHUNK)
            s = jnp.where(prev_side, s2[rows, 0:CHUNK], s2[rows, CHUNK:2 * CHUNK]) + bias_ref[bias_idx, h]
            sink = sink_ref[h] * LOG2E
            m = jnp.maximum(jnp.max(s, axis=-1, keepdims=True), sink)
            p = jnp.exp2(s - m)
            dens.append(jnp.sum(p, axis=-1, keepdims=True) + jnp.exp2(sink - m))
            pb = p.astype(BF16)
            zero = jnp.zeros_like(pb)
            probs.append(jnp.concatenate([jnp.where(prev_side, pb, zero), jnp.where(prev_side, zero, pb)], axis=1))
        o = _dot(jnp.concatenate(probs, axis=0), vk[:, g * HEAD_DIM:(g + 1) * HEAD_DIM])
        for i in range(B_GROUP):
            outs.append(o[i * CHUNK:(i + 1) * CHUNK] * (1.0 / dens[i]))
    return jnp.concatenate(outs, axis=-1)


def _head_stack(xb, lane_head):
    zero = jnp.zeros_like(xb)
    return jnp.concatenate([jnp.where(lane_head == h, xb, zero) for h in range(C_HEADS)], axis=0)


def _hgrn_gates(cq, cf, prm, tri):
    log_lb, log_1m_lb, one_m_lb = prm
    qs = _silu(cq)
    e = jnp.exp(-jnp.abs(cf))
    r = 1.0 / (1.0 + e)
    ls = jnp.minimum(cf, 0.0) - jnp.log(1.0 + e)
    kin = one_m_lb * jnp.where(cf > 0.0, e * r, r)
    bq = log_1m_lb + ls
    mx = jnp.maximum(log_lb, bq)
    mn = jnp.minimum(log_lb, bq)
    lf = mx + jnp.log(1.0 + jnp.exp(mn - mx))
    hi, mid = _split2(lf)
    b = _dot(tri, hi) + _dot(tri, mid)
    return qs, kin, b


def _hgrn_scores_one_ref(qs, kin, b, lvl_ref, lane_head):
    d = b - b[HGRN_REF_ROW:HGRN_REF_ROW + 1, :]
    s = _dot_nt((qs * jnp.exp(d)).astype(BF16), _head_stack((kin * jnp.exp(-d)).astype(BF16), lane_head))
    return jnp.where(lvl_ref[...] >= 0, s, 0.0)


def _hgrn_scores_levels(qs, kin, b, lvl_ref, lane_head, row_id):
    lvl = lvl_ref[...]
    a = jnp.where(lvl == 0, _dot_nt(qs.astype(BF16), _head_stack(kin.astype(BF16), lane_head)), 0.0)
    for li, half in enumerate(HGRN_LEVELS):
        blk = 2 * half
        if blk >= 8:
            pieces = []
            for j in range(CHUNK // blk):
                r = j * blk + half - 1
                pieces.append(jnp.broadcast_to(b[r:r + 1, :], (blk, C_WIDTH)))
            bref = pieces[0] if len(pieces) == 1 else jnp.concatenate(pieces, axis=0)
        else:
            pos = row_id % blk
            bref = b
            for p in range(blk):
                shift = p - (half - 1)
                if shift == 0:
                    continue
                bref = jnp.where(pos == p, pltpu.roll(b, shift % CHUNK, axis=0), bref)
        e = jnp.exp(-jnp.abs(b - bref))
        s = _dot_nt((qs * e).astype(BF16), _head_stack((kin * e).astype(BF16), lane_head))
        a = jnp.where(lvl == li + 1, s, a)
    return a


def _hgrn_finish(a, qs, kin, b, ci, cg, st, og, lane_head, bd):
    total = b[CHUNK - 1:CHUNK, :]
    o = _dot(a, _head_stack(ci.astype(BF16), lane_head))
    o = o + _dot_nt((qs * jnp.exp(b)).astype(BF16), st.astype(BF16))
    kdec = (kin * jnp.exp(total - b)).astype(BF16)
    upd = _dot(ci.T.astype(BF16), kdec)
    vrow_head = lax.broadcasted_iota(jnp.int32, (C_WIDTH, C_WIDTH), 0) // HEAD_DIM
    kcol_head = lax.broadcasted_iota(jnp.int32, (C_WIDTH, C_WIDTH), 1) // HEAD_DIM
    st = st * jnp.exp(total) + jnp.where(vrow_head == kcol_head, upd, 0.0)
    o = o * lax.rsqrt(_seg_mean_sq(o, bd) + EPS) * og
    return o * _silu(cg), st


def _mixer_kernel(x_ref, g1_ref, win_ref, wout_ref, bd_ref, aw_ref, ab_ref, ag_ref, qg_ref, kg_ref,
                  bias_ref, sink_ref, lb_ref, og_ref, lvl_ref, o_ref, kv_ref, st_ref, *, rows):
    j = pl.program_id(1)
    n_sub = rows // SUB_ROWS
    sub_chunks = SUB_ROWS // CHUNK

    @pl.when(j == 0)
    def _():
        kv_ref[...] = jnp.zeros_like(kv_ref)
        st_ref[...] = jnp.zeros_like(st_ref)

    bd = bd_ref[...]
    ab = ab_ref[...]
    lane_group = lax.broadcasted_iota(jnp.int32, (CHUNK, A_WIDTH), 1) // HEAD_DIM
    prev_side = (lax.broadcasted_iota(jnp.int32, (CHUNK, CHUNK), 1)
                 > lax.broadcasted_iota(jnp.int32, (CHUNK, CHUNK), 0))
    lb = lb_ref[...]
    prm = (jnp.log(lb), jnp.log1p(-lb), 1.0 - lb)
    tri = (lax.broadcasted_iota(jnp.int32, (CHUNK, CHUNK), 0)
           >= lax.broadcasted_iota(jnp.int32, (CHUNK, CHUNK), 1)).astype(BF16)
    lane_head = lax.broadcasted_iota(jnp.int32, (CHUNK, C_WIDTH), 1) // HEAD_DIM
    og = og_ref[...]
    st_in = st_ref[...]

    n_chunks = n_sub * sub_chunks
    hs = [None] * n_sub
    tiles = [[None] * PROJ_TILES for _ in range(n_sub)]
    ya, yb, gates, scores = ([None] * n_chunks for _ in range(4))
    carry = {"k": kv_ref[:, 0:B_KV_WIDTH], "v": kv_ref[:, B_KV_WIDTH:2 * B_KV_WIDTH],
             "worst": jnp.zeros((1, C_WIDTH), F32)}
    pending = collections.deque()

    def emit_pending(cost, cost_left):
        for _ in range(-(-len(pending) * cost // cost_left)):
            pending.popleft()()

    def queue_alternating(new):
        old = list(pending)
        pending.clear()
        for k in range(max(len(old), len(new))):
            pending.extend(new[k:k + 1] + old[k:k + 1])

    def load_rows(sb):
        hs[sb] = _rms_rows(x_ref[0, sb * SUB_ROWS:(sb + 1) * SUB_ROWS, :], g1_ref[...]).astype(BF16)

    def project_tile(sb, t):
        tiles[sb][t] = _dot(hs[sb], win_ref[:, t * PROJ_TILE:(t + 1) * PROJ_TILE])

    def tile_rows(sb, t, c):
        return tiles[sb][t][c * CHUNK:(c + 1) * CHUNK]

    def mixer_pieces(sb):
        local = {}

        def prep_a():
            local["u"] = _gelu(tiles[sb][TILE_AU])
            v = _gelu(tiles[sb][TILE_AV])
            local["v"] = (v * lax.rsqrt(_seg_mean_sq(v, bd) + EPS) * ag_ref[...]).astype(BF16)

        def prep_q():
            q = jnp.concatenate([tiles[sb][TILE_BQ], tiles[sb][TILE_BQ + 1]], axis=-1)
            local["q"] = (q * lax.rsqrt(_seg_mean_sq(q, bd) + EPS) * qg_ref[...]).astype(BF16)

        def prep_kv():
            k = tiles[sb][TILE_BKV][:, 0:B_KV_WIDTH]
            local["k"] = (k * lax.rsqrt(_seg_mean_sq(k, bd) + EPS) * kg_ref[...]).astype(BF16)
            local["vv"] = tiles[sb][TILE_BKV][:, B_KV_WIDTH:2 * B_KV_WIDTH].astype(BF16)

        pieces = [(2, prep_a), (1, prep_q), (1, prep_kv)]
        per_chunk = []
        for c in range(sub_chunks):
            i = sb * sub_chunks + c
            sl = slice(c * CHUNK, (c + 1) * CHUNK)

            def gmlp(i=i, sl=sl):
                ya[i] = _gmlp_chunk(local["u"][sl], local["v"][sl], aw_ref, ab, lane_group)

            def attn(i=i, sl=sl):
                kc, vc = local["k"][sl], local["vv"][sl]
                bias_idx = jnp.where(j == 0, 0, 1) if i == 0 else 1
                yb[i] = _attn_chunk(local["q"][sl], jnp.concatenate([carry["k"], kc], axis=0),
                                    jnp.concatenate([carry["v"], vc], axis=0), bias_ref, bias_idx, sink_ref,
                                    prev_side)
                carry["k"], carry["v"] = kc, vc

            def hgrn_gates(i=i, c=c):
                gates[i] = _hgrn_gates(tile_rows(sb, TILE_CQ, c), tile_rows(sb, TILE_CF, c), prm, tri)

            def hgrn_scores(i=i):
                qs, kin, b = gates[i]
                scores[i] = _hgrn_scores_one_ref(qs, kin, b, lvl_ref, lane_head).astype(BF16)
                m = b[HGRN_REF_ROW:HGRN_REF_ROW + 1, :]
                carry["worst"] = jnp.maximum(carry["worst"], jnp.maximum(-m, m - b[CHUNK - 1:CHUNK, :]))

            per_chunk.append((hgrn_gates, gmlp, attn, hgrn_scores))
        pieces.append((2, per_chunk[0][0]))
        for c, (_, gmlp, attn, hgrn_scores) in enumerate(per_chunk):
            pieces += [(1, gmlp), (4, attn)]
            if c + 1 < sub_chunks:
                pieces.append((2, per_chunk[c + 1][0]))
            pieces.append((2, hgrn_scores))
        return pieces

    def finish_pieces(sb, score_list, state):
        mixed = []

        def chunk_out(c):
            i = sb * sub_chunks + c
            qs, kin, b = gates[i]
            yc, state["st"] = _hgrn_finish(score_list[i], qs, kin, b, tile_rows(sb, TILE_CI, c),
                                           tile_rows(sb, TILE_CG, c), state["st"], og, lane_head, bd)
            mixed.append(jnp.concatenate([ya[i], yb[i], yc], axis=-1).astype(BF16))

        def out_tile(t):
            m = mixed[0] if len(mixed) == 1 else jnp.concatenate(mixed, axis=0)
            cols = slice(t * PROJ_TILE, (t + 1) * PROJ_TILE)
            rows_sb = slice(sb * SUB_ROWS, (sb + 1) * SUB_ROWS)
            o_ref[0, rows_sb, cols] = x_ref[0, rows_sb, cols] + _dot(m, wout_ref[:, cols])

        return ([functools.partial(chunk_out, c) for c in range(sub_chunks)]
                + [functools.partial(out_tile, t) for t in range(D_MODEL // PROJ_TILE)])

    state = {"st": st_in}
    load_rows(0)
    for t in range(PROJ_TILES):
        project_tile(0, t)
    for sb in range(n_sub):
        if sb + 1 < n_sub:
            queue_alternating([functools.partial(load_rows, sb + 1)]
                              + [functools.partial(project_tile, sb + 1, t) for t in range(PROJ_TILES)])
        pieces = mixer_pieces(sb)
        cost_left = sum(cost for cost, _ in pieces)
        for cost, piece in pieces:
            piece()
            emit_pending(cost, cost_left)
            cost_left -= cost
        pending.extend(finish_pieces(sb, scores, state))
    while pending:
        pending.popleft()()
    st_ref[...] = state["st"]
    kv_ref[:, 0:B_KV_WIDTH] = carry["k"]
    kv_ref[:, B_KV_WIDTH:2 * B_KV_WIDTH] = carry["v"]

    @pl.when(jnp.max(carry["worst"], axis=-1, keepdims=True)[0, 0] > HGRN_SAFE_LOG_RANGE)
    def _():
        row_id = lax.broadcasted_iota(jnp.int32, (CHUNK, C_WIDTH), 0)
        safe = [_hgrn_scores_levels(qs, kin, b, lvl_ref, lane_head, row_id).astype(BF16) for qs, kin, b in gates]
        redo = {"st": st_in}
        for sb in range(n_sub):
            for piece in finish_pieces(sb, safe, redo):
                piece()
        st_ref[...] = redo["st"]


class _LayerOf(NamedTuple):
    stack: jax.Array
    layer: int


def _operand(c):
    return c.stack if isinstance(c, _LayerOf) else c


def _const_spec(c):
    if isinstance(c, _LayerOf):
        nd = c.stack.ndim - 1
        return pl.BlockSpec((None,) + c.stack.shape[1:], lambda b, j, _l=c.layer, _nd=nd: (_l,) + (0,) * _nd,
                            pipeline_mode=pl.Buffered(1))
    nd = c.ndim
    return pl.BlockSpec(c.shape, lambda b, j, _nd=nd: (0,) * _nd, pipeline_mode=pl.Buffered(1))


def _mixer_layer(x, p, rows):
    bsz, seq, _ = x.shape
    grid = (bsz, seq // rows)
    x_spec = pl.BlockSpec((1, rows, D_MODEL), lambda b, j: (b, j, 0))
    consts = [p["g1"], p["w_in"], p["w_out"], p["bd"], p["a_w"], p["a_b"], p["a_g"], p["q_g"], p["k_g"],
              p["bias"]]
    tail = [p["lb"], p["o_g"], p["lvl"]]
    in_specs = ([x_spec] + [_const_spec(c) for c in consts]
                + [pl.BlockSpec(memory_space=pltpu.SMEM)] + [_const_spec(c) for c in tail])
    return pl.pallas_call(
        functools.partial(_mixer_kernel, rows=rows),
        grid=grid,
        in_specs=in_specs,
        out_specs=x_spec,
        out_shape=jax.ShapeDtypeStruct(x.shape, x.dtype),
        scratch_shapes=[pltpu.VMEM((CHUNK, 2 * B_KV_WIDTH), BF16),
                        pltpu.VMEM((C_WIDTH, C_WIDTH), F32)],
        compiler_params=pltpu.CompilerParams(
            dimension_semantics=("arbitrary", "arbitrary"), vmem_limit_bytes=VMEM_LIMIT_BYTES),
        name="mixer_layer",
    )(x, *map(_operand, consts), p["sinks"], *tail)


def _ffn_kernel(x_ref, g2_ref, wg_ref, wu_ref, cw_ref, cb_ref, wd_ref, o_ref, gate_ref, *, rows):
    j = pl.program_id(1)

    @pl.when(j == 0)
    def _():
        gate_ref[0:8, :] = jnp.zeros((8, D_FF), F32)

    cw = cw_ref[...]
    sub = min(FFN_SUB_ROWS, rows)
    n_sub = rows // sub
    gates, ups = [None] * n_sub, [None] * n_sub

    def hidden(sb):
        r0 = sb * sub
        h = _rms_rows(x_ref[0, r0:r0 + sub, :], g2_ref[...]).astype(BF16)
        gates[sb] = _dot(h, wg_ref[...])
        gate_ref[8 + r0:8 + r0 + sub, :] = gates[sb]
        ups[sb] = _dot(h, wu_ref[...])

    def output(sb):
        r0 = sb * sub
        conv = (cb_ref[...] + cw[2:3, :] * gates[sb] + cw[1:2, :] * gate_ref[7 + r0:7 + r0 + sub, :]
                + cw[0:1, :] * gate_ref[6 + r0:6 + r0 + sub, :])
        act = (_silu(conv) * ups[sb]).astype(BF16)
        o_ref[0, r0:r0 + sub, :] = x_ref[0, r0:r0 + sub, :] + _dot(act, wd_ref[...])

    hidden(0)
    for sb in range(n_sub):
        if sb + 1 < n_sub:
            hidden(sb + 1)
        output(sb)
    gate_ref[0:8, :] = gate_ref[rows:rows + 8, :]


def _ffn_layer(x, p, rows):
    bsz, seq, _ = x.shape
    grid = (bsz, seq // rows)
    x_spec = pl.BlockSpec((1, rows, D_MODEL), lambda b, j: (b, j, 0))
    consts = [p["g2"], p["w_gate"], p["w_up"], p["conv_w"], p["conv_b"], p["w_down"]]
    return pl.pallas_call(
        functools.partial(_ffn_kernel, rows=rows),
        grid=grid,
        in_specs=[x_spec] + [_const_spec(c) for c in consts],
        out_specs=x_spec,
        out_shape=jax.ShapeDtypeStruct(x.shape, x.dtype),
        scratch_shapes=[pltpu.VMEM((rows + 8, D_FF), F32)],
        compiler_params=pltpu.CompilerParams(
            dimension_semantics=("arbitrary", "arbitrary"), vmem_limit_bytes=VMEM_LIMIT_BYTES),
        name="ffn_layer",
    )(x, *map(_operand, consts))


def _t5_causal_bucket_table():
    qi = np.arange(CHUNK)[:, None]
    kj = np.arange(CHUNK)[None, :]
    prev_side = kj > qi
    dist = np.where(prev_side, qi + CHUNK - kj, qi - kj)
    max_exact = REL_BUCKETS // 2
    nf = np.maximum(dist, 1).astype(np.float64)
    large = max_exact + (np.log(nf / max_exact) / math.log(REL_MAX_DIST / max_exact)
                         * (REL_BUCKETS - max_exact)).astype(np.int32)
    large = np.minimum(large, REL_BUCKETS - 1)
    bucket = np.where(dist < max_exact, dist, large).astype(np.int32)
    return bucket, prev_side


def _attention_bias_tables(rel_bias):
    bucket, prev_side = _t5_causal_bucket_table()
    onehot = (jnp.asarray(bucket.reshape(-1))[:, None] == jnp.arange(REL_BUCKETS)[None, :]).astype(F32)
    bias = jnp.einsum("pb,bh->hp", onehot, rel_bias.astype(F32), precision=lax.Precision.HIGHEST)
    bias = bias.reshape(B_Q_HEADS, CHUNK, CHUNK) * LOG2E
    first = jnp.where(jnp.asarray(prev_side)[None], MASK_VALUE, bias)
    return jnp.stack([first, bias])


def _level_table():
    t = np.arange(CHUNK)[:, None]
    s = np.arange(CHUNK)[None, :]
    x = np.bitwise_xor(t, s)
    msb = np.floor(np.log2(np.maximum(x, 1))).astype(np.int32)
    lvl = np.where(s == t, 0, np.where(s < t, len(HGRN_LEVELS) - msb, -1)).astype(np.int32)
    return np.tile(lvl, (1, C_HEADS))


def _block_diag_ones():
    i = np.arange(256)
    return (i[:, None] // HEAD_DIM == i[None, :] // HEAD_DIM).astype(np.float32)


def kernel(x, norm1_g, w_in, gmlp_vnorm_g, gmlp_w_s, gmlp_b_s, q_norm_g, k_norm_g, attn_sinks, rel_bias,
           hgrn_lb_logits, hgrn_onorm_g, w_out, norm2_g, w_gate, w_up, conv_w, conv_b, w_down):
    depth = w_in.shape[0]
    seq = x.shape[1]
    mixer_rows = min(MIXER_ROWS, seq)
    ffn_rows = min(FFN_ROWS, seq)
    lb_cum = jnp.cumsum(jax.nn.softmax(hgrn_lb_logits.astype(F32), axis=0), axis=0)
    lower_bounds = lb_cum - lb_cum[0]
    bias_tables = _attention_bias_tables(rel_bias)
    causal = jnp.tril(jnp.ones((CHUNK, CHUNK), dtype=bool))
    bd = jnp.asarray(_block_diag_ones(), BF16)
    lvl = jnp.asarray(_level_table())
    w_in, w_out, w_gate, w_up, w_down = (w.astype(BF16) for w in (w_in, w_out, w_gate, w_up, w_down))
    for l in range(depth):
        p = {
            "g1": norm1_g[l].reshape(1, D_MODEL),
            "w_in": _LayerOf(w_in, l),
            "w_out": _LayerOf(w_out, l),
            "bd": bd,
            "a_w": jnp.where(causal, gmlp_w_s[l], 0.0).astype(BF16).reshape(A_GROUPS * CHUNK, CHUNK),
            "a_b": jnp.repeat(gmlp_b_s[l].T, HEAD_DIM, axis=1),
            "a_g": gmlp_vnorm_g[l].reshape(1, A_WIDTH),
            "q_g": jnp.tile(q_norm_g[l], B_Q_HEADS).reshape(1, B_WIDTH) * (HEAD_DIM ** -0.5 * LOG2E),
            "k_g": jnp.tile(k_norm_g[l], B_KV_HEADS).reshape(1, B_KV_WIDTH),
            "bias": bias_tables,
            "sinks": attn_sinks[l].astype(F32),
            "lb": lower_bounds[l].reshape(1, C_WIDTH),
            "o_g": jnp.tile(hgrn_onorm_g[l], C_HEADS).reshape(1, C_WIDTH),
            "lvl": lvl,
            "g2": norm2_g[l].reshape(1, D_MODEL),
            "w_gate": _LayerOf(w_gate, l),
            "w_up": _LayerOf(w_up, l),
            "conv_w": conv_w[l],
            "conv_b": conv_b[l].reshape(1, D_FF),
            "w_down": _LayerOf(w_down, l),
        }
        x = _mixer_layer(x, p, mixer_rows)
        x = _ffn_layer(x, p, ffn_rows)
    return x
```

```python
import collections
import functools
import math
from typing import NamedTuple

import numpy as np
import jax
import jax.numpy as jnp
from jax import lax
from jax.experimental import pallas as pl
from jax.experimental.pallas import tpu as pltpu

F32 = jnp.float32
BF16 = jnp.bfloat16

D_MODEL = 1024
HEAD_DIM = 64
CHUNK = 128
A_GROUPS = 4
A_WIDTH = 256
B_Q_HEADS = 8
B_KV_HEADS = 2
B_GROUP = 4
B_WIDTH = 512
B_KV_WIDTH = 128
REL_BUCKETS = 32
REL_MAX_DIST = 128
C_HEADS = 4
C_WIDTH = 256
IN_WIDTH = 2304
D_FF = 2816
CONV_WIDTH = 3
EPS = 1e-6
MASK_VALUE = -1e30
LOG2E = math.log2(math.e)

PROJ_TILE = 256
PROJ_TILES = IN_WIDTH // PROJ_TILE
TILE_AU, TILE_AV, TILE_BQ, TILE_BKV, TILE_CQ, TILE_CF, TILE_CI, TILE_CG = 0, 1, 2, 4, 5, 6, 7, 8

HGRN_LEVELS = (64, 32, 16, 8, 4, 2, 1)
HGRN_REF_ROW = CHUNK // 2 - 1
HGRN_SAFE_LOG_RANGE = 80.0

MIXER_ROWS = 1024
SUB_ROWS = 256
FFN_ROWS = 512
FFN_SUB_ROWS = 256
VMEM_LIMIT_BYTES = 56 * 1024 * 1024

_dot = functools.partial(jnp.dot, preferred_element_type=F32)
_dot_nt = functools.partial(lax.dot_general, dimension_numbers=(((1,), (1,)), ((), ())),
                            preferred_element_type=F32)


def _split2(x):
    hi = x.astype(BF16)
    mid = (x - hi.astype(F32)).astype(BF16)
    return hi, mid


def _seg_mean_sq(x, bd):
    sq = (x * x).astype(BF16)
    w = x.shape[-1]
    outs = []
    for c0 in range(0, w, 256):
        cw = min(256, w - c0)
        outs.append(_dot(sq[:, c0:c0 + cw], bd[0:cw, 0:cw]))
    out = outs[0] if len(outs) == 1 else jnp.concatenate(outs, axis=-1)
    return out * (1.0 / HEAD_DIM)


def _silu(x):
    hx = 0.5 * x
    return hx + hx * jnp.tanh(hx)


def _gelu(x):
    return 0.5 * x * (1.0 + lax.erf(x * (1.0 / math.sqrt(2.0))))


def _rms_rows(x, g):
    ms = jnp.mean(x * x, axis=-1, keepdims=True)
    return x * lax.rsqrt(ms + EPS) * g


def _gmlp_chunk(u, vb, aw_ref, ab, lane_group):
    r = _dot(aw_ref[...], vb)
    sv = r[0:CHUNK]
    for g in range(1, A_GROUPS):
        sv = jnp.where(lane_group == g, r[g * CHUNK:(g + 1) * CHUNK], sv)
    return u * (sv + ab)


def _attn_chunk(qb, kk, vk, bias_ref, bias_idx, sink_ref, prev_side):
    outs = []
    for g in range(B_KV_HEADS):
        heads = range(g * B_GROUP, (g + 1) * B_GROUP)
        q_stack = jnp.concatenate([qb[:, h * HEAD_DIM:(h + 1) * HEAD_DIM] for h in heads], axis=0)
        s2 = _dot_nt(q_stack, kk[:, g * HEAD_DIM:(g + 1) * HEAD_DIM])
        probs, dens = [], []
        for i, h in enumerate(heads):
            rows = slice(i * CHUNK, (i + 1) * CHUNK)
            s = jnp.where(prev_side, s2[rows, 0:CHUNK], s2[rows, CHUNK:2 * CHUNK]) + bias_ref[bias_idx, h]
            sink = sink_ref[h] * LOG2E
            m = jnp.maximum(jnp.max(s, axis=-1, keepdims=True), sink)
            p = jnp.exp2(s - m)
            dens.append(jnp.sum(p, axis=-1, keepdims=True) + jnp.exp2(sink - m))
            pb = p.astype(BF16)
            zero = jnp.zeros_like(pb)
            probs.append(jnp.concatenate([jnp.where(prev_side, pb, zero), jnp.where(prev_side, zero, pb)], axis=1))
        o = _dot(jnp.concatenate(probs, axis=0), vk[:, g * HEAD_DIM:(g + 1) * HEAD_DIM])
        for i in range(B_GROUP):
            outs.append(o[i * CHUNK:(i + 1) * CHUNK] * (1.0 / dens[i]))
    return jnp.concatenate(outs, axis=-1)


def _head_stack(xb, lane_head):
    zero = jnp.zeros_like(xb)
    return jnp.concatenate([jnp.where(lane_head == h, xb, zero) for h in range(C_HEADS)], axis=0)


def _hgrn_gates(cq, cf, prm, tri):
    log_lb, log_1m_lb, one_m_lb = prm
    qs = _silu(cq)
    e = jnp.exp(-jnp.abs(cf))
    r = 1.0 / (1.0 + e)
    ls = jnp.minimum(cf, 0.0) - jnp.log(1.0 + e)
    kin = one_m_lb * jnp.where(cf > 0.0, e * r, r)
    bq = log_1m_lb + ls
    mx = jnp.maximum(log_lb, bq)
    mn = jnp.minimum(log_lb, bq)
    lf = mx + jnp.log(1.0 + jnp.exp(mn - mx))
    hi, mid = _split2(lf)
    b = _dot(tri, jnp.concatenate([hi, mid], axis=0))
    return qs, kin, b


def _hgrn_scores_one_ref(qs, kin, b, lvl_ref, lane_head):
    d = b - b[HGRN_REF_ROW:HGRN_REF_ROW + 1, :]
    s = _dot_nt((qs * jnp.exp(d)).astype(BF16), _head_stack((kin * jnp.exp(-d)).astype(BF16), lane_head))
    return jnp.where(lvl_ref[...] >= 0, s, 0.0)


def _hgrn_scores_levels(qs, kin, b, lvl_ref, lane_head, row_id):
    lvl = lvl_ref[...]
    a = jnp.where(lvl == 0, _dot_nt(qs.astype(BF16), _head_stack(kin.astype(BF16), lane_head)), 0.0)
    for li, half in enumerate(HGRN_LEVELS):
        blk = 2 * half
        if blk >= 8:
            pieces = []
            for j in range(CHUNK // blk):
                r = j * blk + half - 1
                pieces.append(jnp.broadcast_to(b[r:r + 1, :], (blk, C_WIDTH)))
            bref = pieces[0] if len(pieces) == 1 else jnp.concatenate(pieces, axis=0)
        else:
            pos = row_id % blk
            bref = b
            for p in range(blk):
                shift = p - (half - 1)
                if shift == 0:
                    continue
                bref = jnp.where(pos == p, pltpu.roll(b, shift % CHUNK, axis=0), bref)
        e = jnp.exp(-jnp.abs(b - bref))
        s = _dot_nt((qs * e).astype(BF16), _head_stack((kin * e).astype(BF16), lane_head))
        a = jnp.where(lvl == li + 1, s, a)
    return a


def _hgrn_finish(a, qs, kin, b, ci, cg, st, og, lane_head, bd):
    total = b[CHUNK - 1:CHUNK, :]
    o = _dot(a, _head_stack(ci.astype(BF16), lane_head))
    o = o + _dot_nt((qs * jnp.exp(b)).astype(BF16), st.astype(BF16))
    kdec = (kin * jnp.exp(total - b)).astype(BF16)
    upd = _dot(ci.T.astype(BF16), kdec)
    vrow_head = lax.broadcasted_iota(jnp.int32, (C_WIDTH, C_WIDTH), 0) // HEAD_DIM
    kcol_head = lax.broadcasted_iota(jnp.int32, (C_WIDTH, C_WIDTH), 1) // HEAD_DIM
    st = st * jnp.exp(total) + jnp.where(vrow_head == kcol_head, upd, 0.0)
    o = o * lax.rsqrt(_seg_mean_sq(o, bd) + EPS) * og
    return o * _silu(cg), st


def _mixer_kernel(x_ref, g1_ref, win_ref, wout_ref, bd_ref, aw_ref, ab_ref, ag_ref, qg_ref, kg_ref,
                  bias_ref, sink_ref, lb_ref, og_ref, lvl_ref, o_ref, kv_ref, st_ref, *, rows):
    j = pl.program_id(1)
    n_sub = rows // SUB_ROWS
    sub_chunks = SUB_ROWS // CHUNK

    @pl.when(j == 0)
    def _():
        kv_ref[...] = jnp.zeros_like(kv_ref)
        st_ref[...] = jnp.zeros_like(st_ref)

    bd = bd_ref[...]
    ab = ab_ref[...]
    lane_group = lax.broadcasted_iota(jnp.int32, (CHUNK, A_WIDTH), 1) // HEAD_DIM
    prev_side = (lax.broadcasted_iota(jnp.int32, (CHUNK, CHUNK), 1)
                 > lax.broadcasted_iota(jnp.int32, (CHUNK, CHUNK), 0))
    lb = lb_ref[...]
    prm = (jnp.log(lb), jnp.log1p(-lb), 1.0 - lb)
    tri = (lax.broadcasted_iota(jnp.int32, (CHUNK, 2 * CHUNK), 0)
           >= lax.broadcasted_iota(jnp.int32, (CHUNK, 2 * CHUNK), 1) % CHUNK).astype(BF16)
    lane_head = lax.broadcasted_iota(jnp.int32, (CHUNK, C_WIDTH), 1) // HEAD_DIM
    og = og_ref[...]
    st_in = st_ref[...]

    n_chunks = n_sub * sub_chunks
    hs = [None] * n_sub
    tiles = [[None] * PROJ_TILES for _ in range(n_sub)]
    ya, yb, gates, scores = ([None] * n_chunks for _ in range(4))
    carry = {"k": kv_ref[:, 0:B_KV_WIDTH], "v": kv_ref[:, B_KV_WIDTH:2 * B_KV_WIDTH],
             "worst": jnp.zeros((1, C_WIDTH), F32)}
    pending = collections.deque()

    def emit_pending(cost, cost_left):
        for _ in range(-(-len(pending) * cost // cost_left)):
            pending.popleft()()

    def queue_alternating(new):
        old = list(pending)
        pending.clear()
        for k in range(max(len(old), len(new))):
            pending.extend(new[k:k + 1] + old[k:k + 1])

    def load_rows(sb):
        hs[sb] = _rms_rows(x_ref[0, sb * SUB_ROWS:(sb + 1) * SUB_ROWS, :], g1_ref[...]).astype(BF16)

    def project_tile(sb, t):
        tiles[sb][t] = _dot(hs[sb], win_ref[:, t * PROJ_TILE:(t + 1) * PROJ_TILE])

    def tile_rows(sb, t, c):
        return tiles[sb][t][c * CHUNK:(c + 1) * CHUNK]

    def mixer_pieces(sb):
        local = {}

        def prep_a():
            local["u"] = _gelu(tiles[sb][TILE_AU])
            v = _gelu(tiles[sb][TILE_AV])
            local["v"] = (v * lax.rsqrt(_seg_mean_sq(v, bd) + EPS) * ag_ref[...]).astype(BF16)

        def prep_q():
            q = jnp.concatenate([tiles[sb][TILE_BQ], tiles[sb][TILE_BQ + 1]], axis=-1)
            local["q"] = (q * lax.rsqrt(_seg_mean_sq(q, bd) + EPS) * qg_ref[...]).astype(BF16)

        def prep_kv():
            k = tiles[sb][TILE_BKV][:, 0:B_KV_WIDTH]
            local["k"] = (k * lax.rsqrt(_seg_mean_sq(k, bd) + EPS) * kg_ref[...]).astype(BF16)
            local["vv"] = tiles[sb][TILE_BKV][:, B_KV_WIDTH:2 * B_KV_WIDTH].astype(BF16)

        pieces = [(2, prep_a), (1, prep_q), (1, prep_kv)]
        per_chunk = []
        for c in range(sub_chunks):
            i = sb * sub_chunks + c
            sl = slice(c * CHUNK, (c + 1) * CHUNK)

            def gmlp(i=i, sl=sl):
                ya[i] = _gmlp_chunk(local["u"][sl], local["v"][sl], aw_ref, ab, lane_group)

            def attn(i=i, sl=sl):
                kc, vc = local["k"][sl], local["vv"][sl]
                bias_idx = jnp.where(j == 0, 0, 1) if i == 0 else 1
                yb[i] = _attn_chunk(local["q"][sl], jnp.concatenate([carry["k"], kc], axis=0),
                                    jnp.concatenate([carry["v"], vc], axis=0), bias_ref, bias_idx, sink_ref,
                                    prev_side)
                carry["k"], carry["v"] = kc, vc

            def hgrn_gates(i=i, c=c):
                gates[i] = _hgrn_gates(tile_rows(sb, TILE_CQ, c), tile_rows(sb, TILE_CF, c), prm, tri)

            def hgrn_scores(i=i):
                qs, kin, b = gates[i]
                scores[i] = _hgrn_scores_one_ref(qs, kin, b, lvl_ref, lane_head).astype(BF16)
                m = b[HGRN_REF_ROW:HGRN_REF_ROW + 1, :]
                carry["worst"] = jnp.maximum(carry["worst"], jnp.maximum(-m, m - b[CHUNK - 1:CHUNK, :]))

            per_chunk.append((hgrn_gates, gmlp, attn, hgrn_scores))
        pieces.append((2, per_chunk[0][0]))
        for c, (_, gmlp, attn, hgrn_scores) in enumerate(per_chunk):
            pieces += [(1, gmlp), (4, attn)]
            if c + 1 < sub_chunks:
                pieces.append((2, per_chunk[c + 1][0]))
            pieces.append((2, hgrn_scores))
        return pieces

    def finish_pieces(sb, score_list, state):
        mixed = []

        def chunk_out(c):
            i = sb * sub_chunks + c
            qs, kin, b = gates[i]
            yc, state["st"] = _hgrn_finish(score_list[i], qs, kin, b, tile_rows(sb, TILE_CI, c),
                                           tile_rows(sb, TILE_CG, c), state["st"], og, lane_head, bd)
            mixed.append(jnp.concatenate([ya[i], yb[i], yc], axis=-1).astype(BF16))

        def out_tile(t):
            m = mixed[0] if len(mixed) == 1 else jnp.concatenate(mixed, axis=0)
            cols = slice(t * PROJ_TILE, (t + 1) * PROJ_TILE)
            rows_sb = slice(sb * SUB_ROWS, (sb + 1) * SUB_ROWS)
            o_ref[0, rows_sb, cols] = x_ref[0, rows_sb, cols] + _dot(m, wout_ref[:, cols])

        return ([functools.partial(chunk_out, c) for c in range(sub_chunks)]
                + [functools.partial(out_tile, t) for t in range(D_MODEL // PROJ_TILE)])

    state = {"st": st_in}
    load_rows(0)
    for t in range(PROJ_TILES):
        project_tile(0, t)
    for sb in range(n_sub):
        if sb + 1 < n_sub:
            queue_alternating([functools.partial(load_rows, sb + 1)]
                              + [functools.partial(project_tile, sb + 1, t) for t in range(PROJ_TILES)])
        pieces = mixer_pieces(sb)
        cost_left = sum(cost for cost, _ in pieces)
        for cost, piece in pieces:
            piece()
            emit_pending(cost, cost_left)
            cost_left -= cost
        pending.extend(finish_pieces(sb, scores, state))
    while pending:
        pending.popleft()()
    st_ref[...] = state["st"]
    kv_ref[:, 0:B_KV_WIDTH] = carry["k"]
    kv_ref[:, B_KV_WIDTH:2 * B_KV_WIDTH] = carry["v"]

    @pl.when(jnp.max(carry["worst"], axis=-1, keepdims=True)[0, 0] > HGRN_SAFE_LOG_RANGE)
    def _():
        row_id = lax.broadcasted_iota(jnp.int32, (CHUNK, C_WIDTH), 0)
        safe = [_hgrn_scores_levels(qs, kin, b, lvl_ref, lane_head, row_id).astype(BF16) for qs, kin, b in gates]
        redo = {"st": st_in}
        for sb in range(n_sub):
            for piece in finish_pieces(sb, safe, redo):
                piece()
        st_ref[...] = redo["st"]


class _LayerOf(NamedTuple):
    stack: jax.Array
    layer: int


def _operand(c):
    return c.stack if isinstance(c, _LayerOf) else c


def _const_spec(c):
    if isinstance(c, _LayerOf):
        nd = c.stack.ndim - 1
        return pl.BlockSpec((None,) + c.stack.shape[1:], lambda b, j, _l=c.layer, _nd=nd: (_l,) + (0,) * _nd,
                            pipeline_mode=pl.Buffered(1))
    nd = c.ndim
    return pl.BlockSpec(c.shape, lambda b, j, _nd=nd: (0,) * _nd, pipeline_mode=pl.Buffered(1))


def _mixer_layer(x, p, rows):
    bsz, seq, _ = x.shape
    grid = (bsz, seq // rows)
    x_spec = pl.BlockSpec((1, rows, D_MODEL), lambda b, j: (b, j, 0))
    consts = [p["g1"], p["w_in"], p["w_out"], p["bd"], p["a_w"], p["a_b"], p["a_g"], p["q_g"], p["k_g"],
              p["bias"]]
    tail = [p["lb"], p["o_g"], p["lvl"]]
    in_specs = ([x_spec] + [_const_spec(c) for c in consts]
                + [pl.BlockSpec(memory_space=pltpu.SMEM)] + [_const_spec(c) for c in tail])
    return pl.pallas_call(
        functools.partial(_mixer_kernel, rows=rows),
        grid=grid,
        in_specs=in_specs,
        out_specs=x_spec,
        out_shape=jax.ShapeDtypeStruct(x.shape, x.dtype),
        scratch_shapes=[pltpu.VMEM((CHUNK, 2 * B_KV_WIDTH), BF16),
                        pltpu.VMEM((C_WIDTH, C_WIDTH), F32)],
        compiler_params=pltpu.CompilerParams(
            dimension_semantics=("arbitrary", "arbitrary"), vmem_limit_bytes=VMEM_LIMIT_BYTES),
        name="mixer_layer",
    )(x, *map(_operand, consts), p["sinks"], *tail)


def _ffn_kernel(x_ref, g2_ref, wg_ref, wu_ref, cw_ref, cb_ref, wd_ref, o_ref, gate_ref, *, rows):
    j = pl.program_id(1)

    @pl.when(j == 0)
    def _():
        gate_ref[0:8, :] = jnp.zeros((8, D_FF), F32)

    cw = cw_ref[...]
    sub = min(FFN_SUB_ROWS, rows)
    n_sub = rows // sub
    gates, ups = [None] * n_sub, [None] * n_sub

    def hidden(sb):
        r0 = sb * sub
        h = _rms_rows(x_ref[0, r0:r0 + sub, :], g2_ref[...]).astype(BF16)
        gates[sb] = _dot(h, wg_ref[...])
        gate_ref[8 + r0:8 + r0 + sub, :] = gates[sb]
        ups[sb] = _dot(h, wu_ref[...])

    def output(sb):
        r0 = sb * sub
        conv = (cb_ref[...] + cw[2:3, :] * gates[sb] + cw[1:2, :] * gate_ref[7 + r0:7 + r0 + sub, :]
                + cw[0:1, :] * gate_ref[6 + r0:6 + r0 + sub, :])
        act = (_silu(conv) * ups[sb]).astype(BF16)
        o_ref[0, r0:r0 + sub, :] = x_ref[0, r0:r0 + sub, :] + _dot(act, wd_ref[...])

    hidden(0)
    for sb in range(n_sub):
        if sb + 1 < n_sub:
            hidden(sb + 1)
        output(sb)
    gate_ref[0:8, :] = gate_ref[rows:rows + 8, :]


def _ffn_layer(x, p, rows):
    bsz, seq, _ = x.shape
    grid = (bsz, seq // rows)
    x_spec = pl.BlockSpec((1, rows, D_MODEL), lambda b, j: (b, j, 0))
    consts = [p["g2"], p["w_gate"], p["w_up"], p["conv_w"], p["conv_b"], p["w_down"]]
    return pl.pallas_call(
        functools.partial(_ffn_kernel, rows=rows),
        grid=grid,
        in_specs=[x_spec] + [_const_spec(c) for c in consts],
        out_specs=x_spec,
        out_shape=jax.ShapeDtypeStruct(x.shape, x.dtype),
        scratch_shapes=[pltpu.VMEM((rows + 8, D_FF), F32)],
        compiler_params=pltpu.CompilerParams(
            dimension_semantics=("arbitrary", "arbitrary"), vmem_limit_bytes=VMEM_LIMIT_BYTES),
        name="ffn_layer",
    )(x, *map(_operand, consts))


def _t5_causal_bucket_table():
    qi = np.arange(CHUNK)[:, None]
    kj = np.arange(CHUNK)[None, :]
    prev_side = kj > qi
    dist = np.where(prev_side, qi + CHUNK - kj, qi - kj)
    max_exact = REL_BUCKETS // 2
    nf = np.maximum(dist, 1).astype(np.float64)
    large = max_exact + (np.log(nf / max_exact) / math.log(REL_MAX_DIST / max_exact)
                         * (REL_BUCKETS - max_exact)).astype(np.int32)
    large = np.minimum(large, REL_BUCKETS - 1)
    bucket = np.where(dist < max_exact, dist, large).astype(np.int32)
    return bucket, prev_side


def _attention_bias_tables(rel_bias):
    bucket, prev_side = _t5_causal_bucket_table()
    onehot = (jnp.asarray(bucket.reshape(-1))[:, None] == jnp.arange(REL_BUCKETS)[None, :]).astype(F32)
    bias = jnp.einsum("pb,bh->hp", onehot, rel_bias.astype(F32), precision=lax.Precision.HIGHEST)
    bias = bias.reshape(B_Q_HEADS, CHUNK, CHUNK) * LOG2E
    first = jnp.where(jnp.asarray(prev_side)[None], MASK_VALUE, bias)
    return jnp.stack([first, bias])


def _level_table():
    t = np.arange(CHUNK)[:, None]
    s = np.arange(CHUNK)[None, :]
    x = np.bitwise_xor(t, s)
    msb = np.floor(np.log2(np.maximum(x, 1))).astype(np.int32)
    lvl = np.where(s == t, 0, np.where(s < t, len(HGRN_LEVELS) - msb, -1)).astype(np.int32)
    return np.tile(lvl, (1, C_HEADS))


def _block_diag_ones():
    i = np.arange(256)
    return (i[:, None] // HEAD_DIM == i[None, :] // HEAD_DIM).astype(np.float32)


def kernel(x, norm1_g, w_in, gmlp_vnorm_g, gmlp_w_s, gmlp_b_s, q_norm_g, k_norm_g, attn_sinks, rel_bias,
           hgrn_lb_logits, hgrn_onorm_g, w_out, norm2_g, w_gate, w_up, conv_w, conv_b, w_down):
    depth = w_in.shape[0]
    seq = x.shape[1]
    mixer_rows = min(MIXER_ROWS, seq)
    ffn_rows = min(FFN_ROWS, seq)
    lb_cum = jnp.cumsum(jax.nn.softmax(hgrn_lb_logits.astype(F32), axis=0), axis=0)
    lower_bounds = lb_cum - lb_cum[0]
    bias_tables = _attention_bias_tables(rel_bias)
    causal = jnp.tril(jnp.ones((CHUNK, CHUNK), dtype=bool))
    bd = jnp.asarray(_block_diag_ones(), BF16)
    lvl = jnp.asarray(_level_table())
    w_in, w_out, w_gate, w_up, w_down = (w.astype(BF16) for w in (w_in, w_out, w_gate, w_up, w_down))
    for l in range(depth):
        p = {
            "g1": norm1_g[l].reshape(1, D_MODEL),
            "w_in": _LayerOf(w_in, l),
            "w_out": _LayerOf(w_out, l),
            "bd": bd,
            "a_w": jnp.where(causal, gmlp_w_s[l], 0.0).astype(BF16).reshape(A_GROUPS * CHUNK, CHUNK),
            "a_b": jnp.repeat(gmlp_b_s[l].T, HEAD_DIM, axis=1),
            "a_g": gmlp_vnorm_g[l].reshape(1, A_WIDTH),
            "q_g": jnp.tile(q_norm_g[l], B_Q_HEADS).reshape(1, B_WIDTH) * (HEAD_DIM ** -0.5 * LOG2E),
            "k_g": jnp.tile(k_norm_g[l], B_KV_HEADS).reshape(1, B_KV_WIDTH),
            "bias": bias_tables,
            "sinks": attn_sinks[l].astype(F32),
            "lb": lower_bounds[l].reshape(1, C_WIDTH),
            "o_g": jnp.tile(hgrn_onorm_g[l], C_HEADS).reshape(1, C_WIDTH),
            "lvl": lvl,
            "g2": norm2_g[l].reshape(1, D_MODEL),
            "w_gate": _LayerOf(w_gate, l),
            "w_up": _LayerOf(w_up, l),
            "conv_w": conv_w[l],
            "conv_b": conv_b[l].reshape(1, D_FF),
            "w_down": _LayerOf(w_down, l),
        }
        x = _mixer_layer(x, p, mixer_rows)
        x = _ffn_layer(x, p, ffn_rows)
    return x
```
